```python
import jax, jax.numpy as jnp
from jax import lax
import numpy as np


D_MODEL = 2048
BATCH = 2
SEQ = 8192
DEPTH = 1
DEC_BATCH = 16
DEC_SEQ = 32
PAST_LEN = 2048

CHUNK = 64
SB_HEADS = 8
SB_HEAD_DIM = 128
SB_W = SB_HEADS * SB_HEAD_DIM
SB_BLOCK = 128
GLA_HEADS = 4
GLA_DK = 128
GLA_DV = 256
GLA_K_W = GLA_HEADS * GLA_DK
GLA_V_W = GLA_HEADS * GLA_DV
GLA_LR = 16
GATE_TAU = 16.0
MIX_W = SB_W + GLA_V_W
PROJ_W = 3 * SB_W + 2 * GLA_K_W + GLA_V_W + GLA_LR + GLA_V_W
PEER_HEADS = 8
PEER_DKEY = 256
PEER_HALF = PEER_DKEY // 2
N_KEYS = 128
N_EXPERTS = N_KEYS * N_KEYS
PEER_TOPK = 16
PEER_BLOCK = 64
PLE_DIM = 256
EPS = 1e-6

kernel_name = 'hybrid_stickbreak_gla_peer_stream_step'


def _layernorm(x, g, b):
    xf = x.astype(jnp.float32)
    mu = jnp.mean(xf, axis=-1, keepdims=True)
    var = jnp.mean(jnp.square(xf - mu), axis=-1, keepdims=True)
    return ((xf - mu) * lax.rsqrt(var + EPS) * g + b).astype(x.dtype)


def _rmsnorm(x, g):
    xf = x.astype(jnp.float32)
    return xf * lax.rsqrt(jnp.mean(jnp.square(xf), axis=-1, keepdims=True) + EPS) * g.astype(jnp.float32)


def _split_proj(proj):
    sizes = (SB_W, SB_W, SB_W, GLA_K_W, GLA_K_W, GLA_V_W, GLA_LR, GLA_V_W)
    out = []
    start = 0
    for s in sizes:
        out.append(proj[..., start:start + s])
        start += s
    return out


def _stick_breaking(q, k, v, q_pos, k_pos):
    z = jnp.einsum('bqhd,bkhd->bhqk', q, k).astype(jnp.float32) * (q.shape[-1] ** -0.5)
    mask = k_pos[None, :] < q_pos[:, None]
    log_fail = jnp.where(mask, jax.nn.log_sigmoid(-z), 0.0)
    after = lax.cumsum(log_fail, axis=3, reverse=True) - log_fail
    w = jnp.where(mask, jnp.exp(jax.nn.log_sigmoid(z) + after), 0.0)
    return jnp.einsum('bhqk,bkhd->bqhd', w, v.astype(jnp.float32)).astype(v.dtype)


def _sb_prompt(q, k, v):
    B, L, H, d = q.shape
    nb = L // SB_BLOCK
    k_pos = jnp.arange(L, dtype=jnp.int32)
    qb = jnp.moveaxis(q.reshape(B, nb, SB_BLOCK, H, d), 1, 0)
    pb = k_pos.reshape(nb, SB_BLOCK)
    ob = lax.map(lambda a: _stick_breaking(a[0], k, v, a[1], k_pos), (qb, pb))
    return jnp.moveaxis(ob, 0, 1).reshape(B, L, H, d)


def _gla(q, k, v, log_a, S0):
    B, L, H, dk = q.shape
    dv = v.shape[-1]
    c = CHUNK if L % CHUNK == 0 else L
    n = L // c
    f = jnp.float32
    q = (q.astype(f) * dk ** -0.5).reshape(B, n, c, H, dk)
    k = k.astype(f).reshape(B, n, c, H, dk)
    v = v.astype(f).reshape(B, n, c, H, dv)
    b = jnp.cumsum(log_a.astype(f).reshape(B, n, c, H, dk), axis=2)
    b_last = b[:, :, -1:]
    qe = q * jnp.exp(b)
    ke = k * jnp.exp(-b)
    kd = k * jnp.exp(b_last - b)
    causal = jnp.tril(jnp.ones((c, c), dtype=bool))
    att = jnp.where(causal, jnp.einsum('bnthk,bnshk->bnhts', qe, ke), 0.0)
    o_intra = jnp.einsum('bnhts,bnshv->bnthv', att, v)
    dS = jnp.einsum('bnshk,bnshv->bnhkv', kd, v)
    decay = jnp.exp(b_last[:, :, 0])

    def step(S, xs):
        qe_c, dec_c, dS_c = xs
        o = jnp.einsum('bthk,bhkv->bthv', qe_c, S)
        return dec_c[..., None] * S + dS_c, o

    S_fin, o_inter = lax.scan(step, S0.astype(f),
                              (jnp.moveaxis(qe, 1, 0), jnp.moveaxis(decay, 1, 0), jnp.moveaxis(dS, 1, 0)))
    o = o_intra + jnp.moveaxis(o_inter, 0, 1)
    return o.reshape(B, L, H, dv), S_fin


def _peer(h, w_pq, peer_keys, expert_u, expert_v):
    Bsz, L, D = h.shape
    T = Bsz * L
    nblk = -(-T // PEER_BLOCK)
    x = jnp.pad(h.reshape(T, D), ((0, nblk * PEER_BLOCK - T), (0, 0)))

    def block(xb):
        q = (xb @ w_pq).reshape(PEER_BLOCK, PEER_HEADS, 2, PEER_HALF)
        s = jnp.einsum('thpc,hpnc->thpn', q, peer_keys).astype(jnp.float32)
        s_top, i_top = lax.top_k(s, PEER_TOPK)
        cand = (s_top[:, :, 0, :, None] + s_top[:, :, 1, None, :]).reshape(PEER_BLOCK, PEER_HEADS, PEER_TOPK * PEER_TOPK)
        cand_idx = (i_top[:, :, 0, :, None] * N_KEYS + i_top[:, :, 1, None, :]).reshape(PEER_BLOCK, PEER_HEADS, PEER_TOPK * PEER_TOPK)
        best, pos = lax.top_k(cand, PEER_TOPK)
        idx = jnp.take_along_axis(cand_idx, pos, axis=-1).reshape(PEER_BLOCK, PEER_HEADS * PEER_TOPK)
        g = jax.nn.softmax(best, axis=-1).reshape(PEER_BLOCK, PEER_HEADS * PEER_TOPK)
        u = jnp.take(expert_u, idx, axis=0)
        a = jnp.einsum('tkd,td->tk', u, xb).astype(jnp.float32)
        wgt = (jax.nn.gelu(a) * g).astype(xb.dtype)
        vsel = jnp.take(expert_v, idx, axis=0)
        return jnp.einsum('tk,tkd->td', wgt, vsel)

    y = lax.map(block, x.reshape(nblk, PEER_BLOCK, D))
    return y.reshape(-1, D)[:T].reshape(Bsz, L, D)


def _layer(h, pe, past_k, past_v, S0, w_in, w_gla_lr, b_gla_lr, g_sb_norm, g_gla_norm, w_out,
           ln1_g, ln1_b, w_pq, peer_keys, expert_u, expert_v, ln2_g, ln2_b, w_ple, w_ple_gate):
    Bsz, L, _ = h.shape
    alpha = (2.0 * DEPTH) ** 0.25
    proj = h @ w_in
    q_sb, k_sb, v_sb, q_g, k_g, v_g, lr_g, r_g = _split_proj(proj)
    heads = lambda t, nh: t.reshape(Bsz, L, nh, -1)
    q_sb, k_sb, v_sb = heads(q_sb, SB_HEADS), heads(k_sb, SB_HEADS), heads(v_sb, SB_HEADS)
    if past_k is None:
        o_sb = _sb_prompt(q_sb, k_sb, v_sb)
    else:
        P = past_k.shape[1]
        k_all = jnp.concatenate([past_k.astype(k_sb.dtype), k_sb], axis=1)
        v_all = jnp.concatenate([past_v.astype(v_sb.dtype), v_sb], axis=1)
        q_pos = P + jnp.arange(L, dtype=jnp.int32)
        k_pos = jnp.arange(P + L, dtype=jnp.int32)
        o_sb = _stick_breaking(q_sb, k_all, v_all, q_pos, k_pos)
    log_a = jax.nn.log_sigmoid((lr_g @ w_gla_lr + b_gla_lr).astype(jnp.float32)) / GATE_TAU
    o_g, S_new = _gla(heads(q_g, GLA_HEADS), heads(k_g, GLA_HEADS), heads(v_g, GLA_HEADS),
                      heads(log_a, GLA_HEADS), S0)
    o_g = _rmsnorm(o_g, g_gla_norm) * jax.nn.silu(heads(r_g, GLA_HEADS).astype(jnp.float32))
    o_sb = _rmsnorm(o_sb, g_sb_norm)
    mixed = jnp.concatenate([o_sb.reshape(Bsz, L, SB_W), o_g.reshape(Bsz, L, GLA_V_W)], axis=-1).astype(h.dtype)
    h = _layernorm(alpha * h + mixed @ w_out, ln1_g, ln1_b)
    h = _layernorm(alpha * h + _peer(h, w_pq, peer_keys, expert_u, expert_v), ln2_g, ln2_b)
    h = h + jax.nn.sigmoid(h @ w_ple_gate) * (pe @ w_ple)
    return h, k_sb, v_sb, S_new


def setup_inputs(seed: int = 0) -> dict:
    key = jax.random.key(seed)
    ks = jax.random.split(key, 24)
    nrm = lambda k, shape, s: jax.random.normal(k, shape, jnp.float32) * s
    beta = (8.0 * DEPTH) ** -0.25
    return {
        'x_prompt': nrm(ks[0], (BATCH, SEQ, D_MODEL), 1.0),
        'x_sample': nrm(ks[1], (DEC_BATCH, DEC_SEQ, D_MODEL), 1.0),
        'cache_sb_k': nrm(ks[2], (DEPTH, DEC_BATCH, PAST_LEN, SB_HEADS, SB_HEAD_DIM), 1.0),
        'cache_sb_v': nrm(ks[3], (DEPTH, DEC_BATCH, PAST_LEN, SB_HEADS, SB_HEAD_DIM), 1.0),
        'state_gla': nrm(ks[4], (DEPTH, DEC_BATCH, GLA_HEADS, GLA_DK, GLA_DV), 0.5),
        'p_prompt': nrm(ks[5], (DEPTH, BATCH, SEQ, PLE_DIM), 1.0),
        'p_sample': nrm(ks[6], (DEPTH, DEC_BATCH, DEC_SEQ, PLE_DIM), 1.0),
        'w_in': nrm(ks[7], (DEPTH, D_MODEL, PROJ_W), D_MODEL ** -0.5),
        'w_gla_lr': nrm(ks[8], (DEPTH, GLA_LR, GLA_K_W), GLA_LR ** -0.5),
        'b_gla_lr': nrm(ks[9], (DEPTH, GLA_K_W), 0.1),
        'g_sb_norm': 1.0 + nrm(ks[10], (DEPTH, SB_HEAD_DIM), 0.02),
        'g_gla_norm': 1.0 + nrm(ks[11], (DEPTH, GLA_DV), 0.02),
        'w_out': nrm(ks[12], (DEPTH, MIX_W, D_MODEL), beta * MIX_W ** -0.5),
        'ln1_g': 1.0 + nrm(ks[13], (DEPTH, D_MODEL), 0.02),
        'ln1_b': nrm(ks[14], (DEPTH, D_MODEL), 0.02),
        'w_pq': nrm(ks[15], (DEPTH, D_MODEL, PEER_HEADS * PEER_DKEY), D_MODEL ** -0.5),
        'peer_keys': nrm(ks[16], (DEPTH, PEER_HEADS, 2, N_KEYS, PEER_HALF), PEER_HALF ** -0.5),
        'expert_u': nrm(ks[17], (DEPTH, N_EXPERTS, D_MODEL), D_MODEL ** -0.5),
        'expert_v': nrm(ks[18], (DEPTH, N_EXPERTS, D_MODEL), beta),
        'ln2_g': 1.0 + nrm(ks[19], (DEPTH, D_MODEL), 0.02),
        'ln2_b': nrm(ks[20], (DEPTH, D_MODEL), 0.02),
        'w_ple': nrm(ks[21], (DEPTH, PLE_DIM, D_MODEL), PLE_DIM ** -0.5),
        'w_ple_gate': nrm(ks[22], (DEPTH, D_MODEL, D_MODEL), D_MODEL ** -0.5),
    }


def reference(x_prompt, x_sample, cache_sb_k, cache_sb_v, state_gla, p_prompt, p_sample,
              w_in, w_gla_lr, b_gla_lr, g_sb_norm, g_gla_norm, w_out, ln1_g, ln1_b,
              w_pq, peer_keys, expert_u, expert_v, ln2_g, ln2_b, w_ple, w_ple_gate):
    hp = x_prompt
    hs = x_sample
    kp_l, vp_l, sp_l, ks_l, vs_l, ss_l = [], [], [], [], [], []
    for i in range(DEPTH):
        params = (w_in[i], w_gla_lr[i], b_gla_lr[i], g_sb_norm[i], g_gla_norm[i], w_out[i],
                  ln1_g[i], ln1_b[i], w_pq[i], peer_keys[i], expert_u[i], expert_v[i],
                  ln2_g[i], ln2_b[i], w_ple[i], w_ple_gate[i])
        S0p = jnp.zeros((hp.shape[0], GLA_HEADS, GLA_DK, GLA_DV), jnp.float32)
        hp, kp, vp, sp = _layer(hp, p_prompt[i], None, None, S0p, *params)
        hs, ks_, vs_, ss = _layer(hs, p_sample[i], cache_sb_k[i], cache_sb_v[i], state_gla[i], *params)
        kp_l.append(kp)
        vp_l.append(vp)
        sp_l.append(sp)
        ks_l.append(ks_)
        vs_l.append(vs_)
        ss_l.append(ss)
    return (hp, hs, jnp.stack(kp_l), jnp.stack(vp_l), jnp.stack(sp_l),
            jnp.stack(ks_l), jnp.stack(vs_l), jnp.stack(ss_l))
```

```python
import functools

import jax
import jax.numpy as jnp
from jax import lax
from jax.experimental import pallas as pl
from jax.experimental.pallas import tpu as pltpu

BF16 = jnp.bfloat16
F32 = jnp.float32

SB_HEADS = 8
SB_D = 128
SB_W = SB_HEADS * SB_D
GLA_HEADS = 4
GLA_DK = 128
GLA_DV = 256
GLA_K_W = GLA_HEADS * GLA_DK
GLA_V_W = GLA_HEADS * GLA_DV
GLA_LR = 16
GATE_TAU = 16.0
CHUNK = 64
PEER_HEADS = 8
PEER_HALF = 128
N_KEYS = 128
TOPK = 16
EPS = 1e-6

LANE = 128
SUBLANE = 8
VMEM_LIMIT = 56 * 1024 * 1024

COL_QSB = 0
COL_KSB = SB_W // LANE
COL_VSB = 2 * SB_W // LANE
COL_QG = 3 * SB_W // LANE
COL_KG = COL_QG + GLA_K_W // LANE
COL_VG = COL_KG + GLA_K_W // LANE
COL_RG = COL_VG + GLA_V_W // LANE
COL_LR = COL_RG + GLA_V_W // LANE
PROJ_PAD_W = 6400

EXP_ZERO_BELOW = -88.0

_NT = (((1,), (1,)), ((), ()))
_TN = (((0,), (0,)), ((), ()))


def _params(*sem):
    return pltpu.CompilerParams(dimension_semantics=sem, vmem_limit_bytes=VMEM_LIMIT)


def _linear_kernel(x_ref, w_ref, o_ref, xb_ref):
    @pl.when(pl.program_id(1) == 0)
    def _():
        xb_ref[...] = x_ref[...].astype(BF16)

    o_ref[...] = jnp.dot(xb_ref[...], w_ref[...], preferred_element_type=F32)


def _linear(x, w, tm, tn):
    t, k = x.shape
    n = w.shape[1]
    return pl.pallas_call(
        _linear_kernel,
        out_shape=jax.ShapeDtypeStruct((t, n), F32),
        grid=(t // tm, n // tn),
        in_specs=[pl.BlockSpec((tm, k), lambda i, j: (i, 0)), pl.BlockSpec((k, tn), lambda i, j: (0, j))],
        out_specs=pl.BlockSpec((tm, tn), lambda i, j: (i, j)),
        scratch_shapes=[pltpu.VMEM((tm, k), BF16)],
        compiler_params=_params("parallel", "arbitrary"),
        name="in_proj",
    )(x, w)


def _logsig_pair(z):
    l1p = jnp.log1p(jnp.exp(-jnp.abs(z)))
    return jnp.minimum(z, 0.0) - l1p, -jnp.maximum(z, 0.0) - l1p


def _split3(x):
    hi = x.astype(BF16)
    r = x - hi.astype(F32)
    mid = r.astype(BF16)
    lo = (r - mid.astype(F32)).astype(BF16)
    return hi, mid, lo


def _tri(shape, fn):
    row = lax.broadcasted_iota(jnp.int32, shape, 0)
    col = lax.broadcasted_iota(jnp.int32, shape, 1)
    return fn(row, col)


def _sb_block(qb, kblk, vblk, carry, mask, mtri):
    z = lax.dot_general(qb, kblk.astype(BF16), _NT, preferred_element_type=F32) * (SB_D ** -0.5)
    ls, lf = _logsig_pair(z)
    if mask is not None:
        lf = jnp.where(mask, lf, 0.0)
    hi, mid, lo = _split3(lf)
    dot = lambda a: jnp.dot(a, mtri, preferred_element_type=F32)
    after = carry + ((dot(lo) + dot(mid)) + dot(hi))
    w = jnp.exp(ls + after)
    if mask is not None:
        w = jnp.where(mask, w, 0.0)
    pv = jnp.dot(w.astype(BF16), vblk.astype(BF16), preferred_element_type=F32)
    return pv, carry + jnp.sum(lf, axis=1, keepdims=True)


def _sb_attend(qb, kd, vd, kp_ref, vp_ref, n_past, tk, g, o_ref, acc_ref, carry_ref):
    tq = qb.shape[0]
    mask = _tri((tq, tq), lambda r, c: c < r)
    mtri_d = _tri((tq, tq), lambda r, c: jnp.where(r > c, 1.0, 0.0)).astype(BF16)
    pv, carry = _sb_block(qb, kd, vd, jnp.zeros((tq, 1), F32), mask, mtri_d)
    acc_ref[...] = pv
    carry_ref[...] = carry
    mtri = _tri((tk, tk), lambda r, c: jnp.where(r > c, 1.0, 0.0)).astype(BF16)

    def cond(st):
        kb, mx = st
        return jnp.logical_and(kb >= 0, mx > EXP_ZERO_BELOW)

    def body(st):
        kb, _ = st
        s0 = pl.multiple_of(kb * tk, tk)
        pv, c = _sb_block(qb, kp_ref[pl.ds(s0, tk), :], vp_ref[pl.ds(s0, tk), :], carry_ref[...], None, mtri)
        acc_ref[...] += pv
        carry_ref[...] = c
        return kb - 1, jnp.max(c)

    lax.while_loop(cond, body, (n_past - 1, jnp.max(carry)))
    o = acc_ref[...]
    o_ref[...] = o * lax.rsqrt(jnp.mean(o * o, axis=-1, keepdims=True) + EPS) * g


def _sb_prompt_kernel(q_ref, k_ref, v_ref, g_ref, o_ref, acc_ref, carry_ref):
    tq = q_ref.shape[0]
    qi = pl.program_id(2)
    s0 = pl.multiple_of(qi * tq, tq)
    _sb_attend(q_ref[...].astype(BF16), k_ref[pl.ds(s0, tq), :], v_ref[pl.ds(s0, tq), :],
               k_ref, v_ref, qi, tq, g_ref[...], o_ref, acc_ref, carry_ref)


def _sb_prompt(proj, g_sb, batch, seq, tq=128):
    nq = seq // tq
    return pl.pallas_call(
        _sb_prompt_kernel,
        out_shape=jax.ShapeDtypeStruct((batch * seq, SB_W), F32),
        grid=(batch, SB_HEADS, nq),
        in_specs=[
            pl.BlockSpec((tq, SB_D), lambda b, h, i: (b * nq + i, COL_QSB + h)),
            pl.BlockSpec((seq, SB_D), lambda b, h, i: (b, COL_KSB + h)),
            pl.BlockSpec((seq, SB_D), lambda b, h, i: (b, COL_VSB + h)),
            pl.BlockSpec((1, SB_D), lambda b, h, i: (0, 0)),
        ],
        out_specs=pl.BlockSpec((tq, SB_D), lambda b, h, i: (b * nq + i, h)),
        scratch_shapes=[pltpu.VMEM((tq, SB_D), F32), pltpu.VMEM((tq, 1), F32)],
        compiler_params=_params("parallel", "parallel", "arbitrary"),
        name="sb_prompt",
    )(proj, proj, proj, g_sb)


def _sb_sample_kernel(q_ref, kn_ref, vn_ref, kp_ref, vp_ref, g_ref, o_ref, acc_ref, carry_ref, *, tk):
    n_past = kp_ref.shape[0] // tk
    _sb_attend(q_ref[...].astype(BF16), kn_ref[...], vn_ref[...], kp_ref, vp_ref, n_past, tk,
               g_ref[...], o_ref, acc_ref, carry_ref)


def _sb_sample(proj, past_k, past_v, g_sb, batch, seq, tk=128):
    past = past_k.shape[1]
    return pl.pallas_call(
        functools.partial(_sb_sample_kernel, tk=tk),
        out_shape=jax.ShapeDtypeStruct((batch * seq, SB_W), F32),
        grid=(batch, SB_HEADS),
        in_specs=[
            pl.BlockSpec((seq, SB_D), lambda b, h: (b, COL_QSB + h)),
            pl.BlockSpec((seq, SB_D), lambda b, h: (b, COL_KSB + h)),
            pl.BlockSpec((seq, SB_D), lambda b, h: (b, COL_VSB + h)),
            pl.BlockSpec((None, past, SB_D), lambda b, h: (b, 0, h)),
            pl.BlockSpec((None, past, SB_D), lambda b, h: (b, 0, h)),
            pl.BlockSpec((1, SB_D), lambda b, h: (0, 0)),
        ],
        out_specs=pl.BlockSpec((seq, SB_D), lambda b, h: (b, h)),
        scratch_shapes=[pltpu.VMEM((seq, SB_D), F32), pltpu.VMEM((seq, 1), F32)],
        compiler_params=_params("parallel", "parallel"),
        name="sb_sample",
    )(proj, proj, proj, past_k, past_v, g_sb)


def _gla_kernel(q_ref, k_ref, v_ref, r_ref, lr_ref, wlr_ref, blr_ref, gn_ref, s0_ref, o_ref, sout_ref, st_ref, *, c):
    si = pl.program_id(2)
    n_chunks = q_ref.shape[0] // c

    @pl.when(si == 0)
    def _():
        st_ref[...] = s0_ref[...].T

    ltri = _tri((c, c), lambda r, cc: jnp.where(cc <= r, 1.0, 0.0)).astype(BF16)
    causal = _tri((c, c), lambda r, cc: cc <= r)

    def chunk(j, carry):
        sl = pl.ds(pl.multiple_of(j * c, c), c)
        pre = jnp.dot(lr_ref[sl, :].astype(BF16), wlr_ref[...], preferred_element_type=F32) + blr_ref[...]
        la = (jnp.minimum(pre, 0.0) - jnp.log1p(jnp.exp(-jnp.abs(pre)))) * (1.0 / GATE_TAU)
        hi, mid, lo = _split3(la)
        dot = lambda a: jnp.dot(ltri, a, preferred_element_type=F32)
        b = (dot(lo) + dot(mid)) + dot(hi)
        b_last = b[c - 1:c, :]
        q = q_ref[sl, :] * (GLA_DK ** -0.5)
        k = k_ref[sl, :]
        vb = v_ref[sl, :].astype(BF16)
        qe = (q * jnp.exp(b)).astype(BF16)
        ke = (k * jnp.exp(-b)).astype(BF16)
        kd = (k * jnp.exp(b_last - b)).astype(BF16)
        att = lax.dot_general(qe, ke, _NT, preferred_element_type=F32)
        att = jnp.where(causal, att, 0.0).astype(BF16)
        st = st_ref[...]
        o = jnp.dot(att, vb, preferred_element_type=F32)
        o = o + lax.dot_general(qe, st.astype(BF16), _NT, preferred_element_type=F32)
        st_ref[...] = st * jnp.exp(b_last) + lax.dot_general(vb, kd, _TN, preferred_element_type=F32)
        on = o * lax.rsqrt(jnp.mean(o * o, axis=-1, keepdims=True) + EPS) * gn_ref[...]
        r = r_ref[sl, :]
        o_ref[sl, :] = on * (r * (1.0 / (1.0 + jnp.exp(-r))))
        return carry

    lax.fori_loop(0, n_chunks, chunk, 0)

    @pl.when(si == pl.num_programs(2) - 1)
    def _():
        sout_ref[...] = st_ref[...].T


def _gla(proj, wlr, blr, g_gla, s0, batch, seq):
    c = CHUNK if seq % CHUNK == 0 else seq
    rows = min(seq, 8 * c)
    ns = seq // rows
    row = lambda b, h, s: b * ns + s
    return pl.pallas_call(
        functools.partial(_gla_kernel, c=c),
        out_shape=(jax.ShapeDtypeStruct((batch * seq, GLA_V_W), F32),
                   jax.ShapeDtypeStruct((batch, GLA_HEADS, GLA_DK, GLA_DV), F32)),
        grid=(batch, GLA_HEADS, ns),
        in_specs=[
            pl.BlockSpec((rows, GLA_DK), lambda b, h, s: (row(b, h, s), COL_QG + h)),
            pl.BlockSpec((rows, GLA_DK), lambda b, h, s: (row(b, h, s), COL_KG + h)),
            pl.BlockSpec((rows, GLA_DV), lambda b, h, s: (row(b, h, s), COL_VG // 2 + h)),
            pl.BlockSpec((rows, GLA_DV), lambda b, h, s: (row(b, h, s), COL_RG // 2 + h)),
            pl.BlockSpec((rows, LANE), lambda b, h, s: (row(b, h, s), COL_LR)),
            pl.BlockSpec((LANE, GLA_DK), lambda b, h, s: (0, h)),
            pl.BlockSpec((1, GLA_DK), lambda b, h, s: (0, h)),
            pl.BlockSpec((1, GLA_DV), lambda b, h, s: (0, 0)),
            pl.BlockSpec((None, None, GLA_DK, GLA_DV), lambda b, h, s: (b, h, 0, 0)),
        ],
        out_specs=(pl.BlockSpec((rows, GLA_DV), lambda b, h, s: (row(b, h, s), h)),
                   pl.BlockSpec((None, None, GLA_DK, GLA_DV), lambda b, h, s: (b, h, 0, 0))),
        scratch_shapes=[pltpu.VMEM((GLA_DV, GLA_DK), F32)],
        compiler_params=_params("parallel", "parallel", "arbitrary"),
        name="gla",
    )(proj, proj, proj, proj, proj, wlr, blr, g_gla, s0)


def _layernorm(y, g, b):
    mu = jnp.mean(y, axis=-1, keepdims=True)
    d = y - mu
    var = jnp.mean(d * d, axis=-1, keepdims=True)
    return d * lax.rsqrt(var + EPS) * g + b


def _outproj_kernel(osb_ref, og_ref, x_ref, w_ref, g_ref, b_ref, h_ref, hb_ref, *, alpha):
    acc = jnp.dot(osb_ref[...].astype(BF16), w_ref[0:SB_W, :], preferred_element_type=F32)
    acc = acc + jnp.dot(og_ref[...].astype(BF16), w_ref[SB_W:SB_W + GLA_V_W, :], preferred_element_type=F32)
    h = _layernorm(alpha * x_ref[...] + acc, g_ref[...], b_ref[...])
    h_ref[...] = h
    hb_ref[...] = h.astype(BF16)


def _outproj(o_sb, o_g, x, w_out, ln_g, ln_b, alpha, tm=256):
    t, d = x.shape
    tm = min(tm, t)
    tok = lambda w: pl.BlockSpec((tm, w), lambda i: (i, 0))
    full = lambda a: pl.BlockSpec(a.shape, lambda i: (0,) * a.ndim)
    return pl.pallas_call(
        functools.partial(_outproj_kernel, alpha=alpha),
        out_shape=(jax.ShapeDtypeStruct((t, d), F32), jax.ShapeDtypeStruct((t, d), BF16)),
        grid=(t // tm,),
        in_specs=[tok(SB_W), tok(GLA_V_W), tok(d), full(w_out), full(ln_g), full(ln_b)],
        out_specs=(tok(d), tok(d)),
        compiler_params=_params("parallel"),
        name="out_proj_ln1",
    )(o_sb, o_g, x, w_out, ln_g, ln_b)


def _top16(s, iota, vals_ref):
    rank = jnp.full(s.shape, float(TOPK + 1), F32)
    for r in range(TOPK):
        m = jnp.max(s, axis=0, keepdims=True)
        idx = jnp.min(jnp.where(s == m, iota, float(N_KEYS)), axis=0, keepdims=True)
        sel = iota == idx
        rank = jnp.where(sel, float(r + 1), rank)
        s = jnp.where(sel, -jnp.inf, s)
        vals_ref[r:r + 1, :] = m
    return rank


def _peer_topk_kernel(hb_ref, wpq_ref, keys_ref, rank2_ref, rfull_ref, e1_ref, e2_ref,
                      s_ref, a_ref, b_ref, c_ref, best_ref, rv_ref):
    tm = hb_ref.shape[0]
    q = jnp.dot(hb_ref[...], wpq_ref[...], preferred_element_type=F32).astype(BF16)
    for hp in range(2 * PEER_HEADS):
        s_ref[hp] = lax.dot_general(keys_ref[hp], q[:, hp * PEER_HALF:(hp + 1) * PEER_HALF], _NT,
                                    preferred_element_type=F32)
    iota_k = lax.broadcasted_iota(jnp.int32, (N_KEYS, LANE), 0).astype(F32)
    iota_c = lax.broadcasted_iota(jnp.int32, (64, LANE), 0).astype(F32)

    for g in range(tm // LANE):
        lanes = pl.ds(g * LANE, LANE)

        def head(h, carry):
            s1 = s_ref[2 * h, :, lanes]
            s2 = s_ref[2 * h + 1, :, lanes]
            rank1 = _top16(s1, iota_k, a_ref)
            rank2 = _top16(s2, iota_k, b_ref)
            c_ref[0:8, :] = a_ref[0:1, :] + b_ref[0:8, :]
            c_ref[8:16, :] = a_ref[0:1, :] + b_ref[8:16, :]
            c_ref[16:24, :] = a_ref[1:2, :] + b_ref[0:8, :]
            c_ref[24:32, :] = a_ref[2:3, :] + b_ref[0:8, :]
            c_ref[32:40, :] = a_ref[3:4, :] + b_ref[0:8, :]
            c_ref[40:44, :] = a_ref[4:5, :] + b_ref[0:4, :]
            c_ref[44:48, :] = a_ref[5:6, :] + b_ref[0:4, :]
            c_ref[48:52, :] = a_ref[6:7, :] + b_ref[0:4, :]
            c_ref[52:56, :] = a_ref[7:8, :] + b_ref[0:4, :]
            c_ref[56:64, :] = a_ref[8:16, :] + b_ref[0:1, :]
            cand = c_ref[...]
            for r in range(TOPK):
                m = jnp.max(cand, axis=0, keepdims=True)
                idx = jnp.min(jnp.where(cand == m, iota_c, 64.0), axis=0, keepdims=True)
                cand = jnp.where(iota_c == idx, -jnp.inf, cand)
                best_ref[r:r + 1, :] = m
            c_ref[...] = jnp.where(cand == -jnp.inf, 1.0, 0.0)
            rsum = lambda lo, hi: jnp.sum(c_ref[lo:hi, :], axis=0, keepdims=True)
            rv_ref[0:1, :] = rsum(0, 16)
            rv_ref[1:2, :] = rsum(16, 24)
            rv_ref[2:3, :] = rsum(24, 32)
            rv_ref[3:4, :] = rsum(32, 40)
            rv_ref[4:5, :] = rsum(40, 44)
            rv_ref[5:6, :] = rsum(44, 48)
            rv_ref[6:7, :] = rsum(48, 52)
            rv_ref[7:8, :] = rsum(52, 56)
            rv_ref[8:16, :] = c_ref[56:64, :]
            rfull = jnp.zeros((N_KEYS, LANE), F32)
            for r in range(TOPK):
                rfull = jnp.where(rank1 == float(r + 1), rv_ref[r:r + 1, :], rfull)
            best = best_ref[...]
            zsum = jnp.sum(jnp.exp(best - best[0:1, :]), axis=0, keepdims=True)
            rank2_ref[h, :, lanes] = rank2
            rfull_ref[h, :, lanes] = rfull
            e1_ref[h, :, lanes] = jnp.exp(s1 - a_ref[0:1, :]) / zsum
            e2_ref[h, :, lanes] = jnp.exp(s2 - b_ref[0:1, :])
            return carry

        lax.fori_loop(0, PEER_HEADS, head, 0)


def _peer_topk(hb, w_pq, keys, tm=256):
    t, d = hb.shape
    tm = min(tm, t)
    full = lambda a: pl.BlockSpec(a.shape, lambda i: (0,) * a.ndim)
    sel = jax.ShapeDtypeStruct((PEER_HEADS, N_KEYS, t), F32)
    sel_spec = pl.BlockSpec((PEER_HEADS, N_KEYS, tm), lambda i: (0, 0, i))
    row16 = pltpu.VMEM((TOPK, LANE), F32)
    return pl.pallas_call(
        _peer_topk_kernel,
        out_shape=(sel, sel, sel, sel),
        grid=(t // tm,),
        in_specs=[pl.BlockSpec((tm, d), lambda i: (i, 0)), full(w_pq), full(keys)],
        out_specs=(sel_spec, sel_spec, sel_spec, sel_spec),
        scratch_shapes=[pltpu.VMEM((2 * PEER_HEADS, N_KEYS, tm), F32), row16, row16,
                        pltpu.VMEM((64, LANE), F32), row16, row16],
        compiler_params=_params("parallel"),
        name="peer_topk",
    )(hb, w_pq, keys)


def _peer_dense_kernel(u_ref, vt_ref, hb_ref, h_ref, rank2_ref, rfull_ref, e1_ref, e2_ref, g_ref, b_ref,
                       o_ref, yt_ref, at_ref, w_ref, *, alpha):
    e = pl.program_id(1)
    te, tm = at_ref.shape
    ni = te // N_KEYS

    @pl.when(e == 0)
    def _():
        yt_ref[...] = jnp.zeros_like(yt_ref)

    at_ref[...] = lax.dot_general(u_ref[...], hb_ref[...], _NT, preferred_element_type=F32)
    igroup = pl.ds(pl.multiple_of(e * ni, SUBLANE), ni)
    for lg in range(tm // LANE):
        lanes = pl.ds(lg * LANE, LANE)
        for il in range(ni):
            rows = pl.ds(il * N_KEYS, N_KEYS)
            gate = jnp.zeros((N_KEYS, LANE), F32)
            for h in range(PEER_HEADS):
                thr = rfull_ref[h, igroup, lanes][il:il + 1, :]
                coef = e1_ref[h, igroup, lanes][il:il + 1, :]
                gate = gate + jnp.where(rank2_ref[h, :, lanes] <= thr, e2_ref[h, :, lanes] * coef, 0.0)
            w_ref[rows, lanes] = (jax.nn.gelu(at_ref[rows, lanes], approximate=True) * gate).astype(BF16)
    yt_ref[...] += jnp.dot(vt_ref[...], w_ref[...], preferred_element_type=F32)

    @pl.when(e == pl.num_programs(1) - 1)
    def _():
        o_ref[...] = _layernorm(alpha * h_ref[...] + yt_ref[...].T, g_ref[...], b_ref[...])


def _peer_dense(u, vt, hb, h, sel, ln_g, ln_b, alpha, tm=512, te=SUBLANE * N_KEYS):
    t, d = h.shape
    ne = u.shape[0]
    tm = min(tm, t)
    once = pl.Buffered(1)
    sel_spec = pl.BlockSpec((PEER_HEADS, N_KEYS, tm), lambda i, e: (0, 0, i), pipeline_mode=once)
    tok_in = pl.BlockSpec((tm, d), lambda i, e: (i, 0), pipeline_mode=once)
    tok = pl.BlockSpec((tm, d), lambda i, e: (i, 0))
    vec = pl.BlockSpec((1, d), lambda i, e: (0, 0))
    return pl.pallas_call(
        functools.partial(_peer_dense_kernel, alpha=alpha),
        out_shape=jax.ShapeDtypeStruct((t, d), F32),
        grid=(t // tm, ne // te),
        in_specs=[pl.BlockSpec((te, d), lambda i, e: (e, 0)), pl.BlockSpec((d, te), lambda i, e: (0, e)),
                  tok_in, tok_in, sel_spec, sel_spec, sel_spec, sel_spec, vec, vec],
        out_specs=tok,
        scratch_shapes=[pltpu.VMEM((d, tm), F32), pltpu.VMEM((te, tm), F32), pltpu.VMEM((te, tm), BF16)],
        compiler_params=_params("parallel", "arbitrary"),
        name="peer_dense_ln2",
    )(u, vt, hb, h, *sel, ln_g, ln_b)


def _ple_kernel(h_ref, pe_ref, wg_ref, wp_ref, o_ref):
    h = h_ref[...]
    gate = jnp.dot(h.astype(BF16), wg_ref[...], preferred_element_type=F32)
    emb = jnp.dot(pe_ref[...].astype(BF16), wp_ref[...], preferred_element_type=F32)
    o_ref[...] = h + (1.0 / (1.0 + jnp.exp(-gate))) * emb


def _ple(h, pe, w_gate, w_ple, tm=256):
    t, d = h.shape
    tm = min(tm, t)
    p = pe.shape[1]
    full = lambda a: pl.BlockSpec(a.shape, lambda i: (0,) * a.ndim)
    return pl.pallas_call(
        _ple_kernel,
        out_shape=jax.ShapeDtypeStruct((t, d), F32),
        grid=(t // tm,),
        in_specs=[pl.BlockSpec((tm, d), lambda i: (i, 0)), pl.BlockSpec((tm, p), lambda i: (i, 0)),
                  full(w_gate), full(w_ple)],
        out_specs=pl.BlockSpec((tm, d), lambda i: (i, 0)),
        compiler_params=_params("parallel"),
        name="ple_gate",
    )(h, pe, w_gate, w_ple)


def _layer(x, pe, past_k, past_v, s0, wts, alpha):
    batch, seq, d = x.shape
    t = batch * seq
    x2 = x.reshape(t, d)
    proj = _linear(x2, wts["w_in"], min(512, t), 1280)
    k_sb = proj[:, SB_W:2 * SB_W].reshape(batch, seq, SB_HEADS, SB_D)
    v_sb = proj[:, 2 * SB_W:3 * SB_W].reshape(batch, seq, SB_HEADS, SB_D)
    if past_k is None:
        o_sb = _sb_prompt(proj, wts["g_sb"], batch, seq)
    else:
        past = past_k.shape[1]
        o_sb = _sb_sample(proj, past_k.reshape(batch, past, SB_W), past_v.reshape(batch, past, SB_W),
                          wts["g_sb"], batch, seq)
    o_g, s_new = _gla(proj, wts["w_lr"], wts["b_lr"], wts["g_gla"], s0, batch, seq)
    h1, h1b = _outproj(o_sb, o_g, x2, wts["w_out"], wts["ln1_g"], wts["ln1_b"], alpha)
    sel = _peer_topk(h1b, wts["w_pq"], wts["keys"])
    h2 = _peer_dense(wts["u"], wts["vt"], h1b, h1, sel, wts["ln2_g"], wts["ln2_b"], alpha)
    out = _ple(h2, pe.reshape(t, -1), wts["w_gate"], wts["w_ple"])
    return out.reshape(batch, seq, d), k_sb, v_sb, s_new


def _prep_weights(w_in, w_gla_lr, b_gla_lr, g_sb_norm, g_gla_norm, w_out, ln1_g, ln1_b,
                  w_pq, peer_keys, expert_u, expert_v, ln2_g, ln2_b, w_ple, w_ple_gate):
    d = w_in.shape[0]
    lr0 = 3 * SB_W + 2 * GLA_K_W + GLA_V_W
    w_fused = jnp.concatenate(
        [w_in[:, :lr0], w_in[:, lr0 + GLA_LR:], w_in[:, lr0:lr0 + GLA_LR],
         jnp.zeros((d, PROJ_PAD_W - w_in.shape[1]), w_in.dtype)], axis=1).astype(BF16)
    w_lr = jnp.concatenate([w_gla_lr, jnp.zeros((LANE - GLA_LR, GLA_K_W), w_gla_lr.dtype)], axis=0).astype(BF16)
    row = lambda a: a.reshape(1, -1)
    return {
        "w_in": w_fused, "w_lr": w_lr, "b_lr": row(b_gla_lr), "g_sb": row(g_sb_norm), "g_gla": row(g_gla_norm),
        "w_out": w_out.astype(BF16), "ln1_g": row(ln1_g), "ln1_b": row(ln1_b),
        "w_pq": w_pq.astype(BF16), "keys": peer_keys.reshape(2 * PEER_HEADS, N_KEYS, PEER_HALF).astype(BF16),
        "u": expert_u.astype(BF16), "vt": expert_v.astype(BF16).T,
        "ln2_g": row(ln2_g), "ln2_b": row(ln2_b), "w_ple": w_ple.astype(BF16), "w_gate": w_ple_gate.astype(BF16),
    }


def kernel(x_prompt, x_sample, cache_sb_k, cache_sb_v, state_gla, p_prompt, p_sample, w_in, w_gla_lr, b_gla_lr, g_sb_norm, g_gla_norm, w_out, ln1_g, ln1_b, w_pq, peer_keys, expert_u, expert_v, ln2_g, ln2_b, w_ple, w_ple_gate):
    depth = w_in.shape[0]
    alpha = (2.0 * depth) ** 0.25
    hp, hs = x_prompt, x_sample
    outs = [[] for _ in range(6)]
    for i in range(depth):
        wts = _prep_weights(w_in[i], w_gla_lr[i], b_gla_lr[i], g_sb_norm[i], g_gla_norm[i], w_out[i],
                            ln1_g[i], ln1_b[i], w_pq[i], peer_keys[i], expert_u[i], expert_v[i],
                            ln2_g[i], ln2_b[i], w_ple[i], w_ple_gate[i])
        s0p = jnp.zeros((hp.shape[0], GLA_HEADS, GLA_DK, GLA_DV), F32)
        hp, kp, vp, sp = _layer(hp, p_prompt[i], None, None, s0p, wts, alpha)
        hs, ks, vs, ss = _layer(hs, p_sample[i], cache_sb_k[i], cache_sb_v[i], state_gla[i], wts, alpha)
        for lst, val in zip(outs, (kp, vp, sp, ks, vs, ss)):
            lst.append(val)
    return (hp, hs) + tuple(jnp.stack(lst) for lst in outs)
```

```python
import functools

import jax
import jax.numpy as jnp
from jax import lax
from jax.experimental import pallas as pl
from jax.experimental.pallas import tpu as pltpu

BF16 = jnp.bfloat16
F32 = jnp.float32

SB_HEADS = 8
SB_D = 128
SB_W = SB_HEADS * SB_D
GLA_HEADS = 4
GLA_DK = 128
GLA_DV = 256
GLA_K_W = GLA_HEADS * GLA_DK
GLA_V_W = GLA_HEADS * GLA_DV
GLA_LR = 16
GATE_TAU = 16.0
CHUNK = 64
PEER_HEADS = 8
PEER_HALF = 128
N_KEYS = 128
TOPK = 16
EPS = 1e-6

LANE = 128
SUBLANE = 8
BF16_ROWS = 16
MXU_N = 256
VMEM_LIMIT = 56 * 1024 * 1024

COL_QSB = 0
COL_QG = SB_W // LANE
COL_KG = COL_QG + GLA_K_W // LANE
COL_VG = COL_KG + GLA_K_W // LANE
COL_RG = COL_VG + GLA_V_W // LANE
REST_W = SB_W + 2 * GLA_K_W + 2 * GLA_V_W
PROJ_TILE = 1024

EXP_ZERO_BELOW = -88.0

CODE_BASE = -(2.0 ** 127)
CODE_STEP = 2.0 ** 110
CODE_BELOW = -(2.0 ** 126)

_NT = (((1,), (1,)), ((), ()))
_TN = (((0,), (0,)), ((), ()))


def _params(*sem):
    return pltpu.CompilerParams(dimension_semantics=sem, vmem_limit_bytes=VMEM_LIMIT)


def _full(a):
    return pl.BlockSpec(a.shape, lambda *_: (0,) * a.ndim)


def _in_proj_kernel(x_ref, w_ref, wlr_ref, k_ref, v_ref, kb_ref, vb_ref, rest_ref, lr_ref, xb_ref):
    j = pl.program_id(1)
    dot = lambda w: jnp.dot(xb_ref[...], w[...], preferred_element_type=F32)

    @pl.when(j == 0)
    def _():
        xb_ref[...] = x_ref[...].astype(BF16)
        lr_ref[...] = dot(wlr_ref)
        acc = dot(w_ref)
        k_ref[...] = acc
        kb_ref[...] = acc.astype(BF16)

    @pl.when(j == 1)
    def _():
        acc = dot(w_ref)
        v_ref[...] = acc
        vb_ref[...] = acc.astype(BF16)

    @pl.when(j >= 2)
    def _():
        rest_ref[...] = dot(w_ref)


def _in_proj(x, w_main, w_lr, tm=512):
    t, d = x.shape
    tm = min(tm, t)
    n_tiles = w_main.shape[1] // PROJ_TILE
    tok = lambda w: pl.BlockSpec((tm, w), lambda i, j: (i, 0))
    sds = lambda w, dt: jax.ShapeDtypeStruct((t, w), dt)
    return pl.pallas_call(
        _in_proj_kernel,
        out_shape=(sds(SB_W, F32), sds(SB_W, F32), sds(SB_W, BF16), sds(SB_W, BF16), sds(REST_W, F32), sds(LANE, F32)),
        grid=(t // tm, n_tiles),
        in_specs=[tok(d), pl.BlockSpec((d, PROJ_TILE), lambda i, j: (0, j)), pl.BlockSpec((d, LANE), lambda i, j: (0, 0))],
        out_specs=(tok(SB_W), tok(SB_W), tok(SB_W), tok(SB_W),
                   pl.BlockSpec((tm, PROJ_TILE), lambda i, j: (i, jnp.maximum(j - 2, 0))), tok(LANE)),
        scratch_shapes=[pltpu.VMEM((tm, d), BF16)],
        compiler_params=_params("parallel", "arbitrary"),
        name="in_proj",
    )(x, w_main, w_lr)


def _logsig_pair(z):
    l1p = jnp.log1p(jnp.exp(-jnp.abs(z)))
    return jnp.minimum(z, 0.0) - l1p, -jnp.maximum(z, 0.0) - l1p


def _split3(x):
    hi = x.astype(BF16)
    r = x - hi.astype(F32)
    mid = r.astype(BF16)
    lo = (r - mid.astype(F32)).astype(BF16)
    return hi, mid, lo


def _tri(shape, fn):
    row = lax.broadcasted_iota(jnp.int32, shape, 0)
    col = lax.broadcasted_iota(jnp.int32, shape, 1)
    return fn(row, col)


def _sb_blocks(qbs, ks, vs, carries, mask, mtri):
    zs = [lax.dot_general(q, k.astype(BF16), _NT, preferred_element_type=F32) * (SB_D ** -0.5)
          for q, k in zip(qbs, ks)]
    pairs = [_logsig_pair(z) for z in zs]
    lfs = [lf if mask is None else jnp.where(mask, lf, 0.0) for _, lf in pairs]
    splits = [_split3(lf) for lf in lfs]
    dot = lambda a: jnp.dot(a, mtri, preferred_element_type=F32)
    cums = [(dot(lo) + dot(mid)) + dot(hi) for hi, mid, lo in splits]
    ws = [jnp.exp(ls + (c + cum)) for (ls, _), c, cum in zip(pairs, carries, cums)]
    if mask is not None:
        ws = [jnp.where(mask, w, 0.0) for w in ws]
    pvs = [jnp.dot(w.astype(BF16), v.astype(BF16), preferred_element_type=F32) for w, v in zip(ws, vs)]
    return pvs, [c + jnp.sum(lf, axis=1, keepdims=True) for c, lf in zip(carries, lfs)]


def _sb_attend(q_ref, kd_ref, vd_ref, d0, kp_ref, vp_ref, n_past, tk, g, o_ref, acc_ref, carry_ref):
    tq = q_ref.shape[0]
    heads = range(SB_HEADS)
    cols = lambda h: slice(h * SB_D, (h + 1) * SB_D)
    qbs = [q_ref[:, cols(h)].astype(BF16) for h in heads]
    mask = _tri((tq, tq), lambda r, c: c < r)
    mtri_d = _tri((tq, tq), lambda r, c: jnp.where(r > c, 1.0, 0.0)).astype(BF16)
    drows = pl.ds(d0, tq)
    pvs, carries = _sb_blocks(qbs, [kd_ref[drows, cols(h)] for h in heads], [vd_ref[drows, cols(h)] for h in heads],
                              [jnp.zeros((tq, 1), F32)] * SB_HEADS, mask, mtri_d)
    for h in heads:
        acc_ref[:, cols(h)] = pvs[h]
        carry_ref[h] = carries[h]
    mtri = _tri((tk, tk), lambda r, c: jnp.where(r > c, 1.0, 0.0)).astype(BF16)

    def cond(st):
        kb, mx = st
        return jnp.logical_and(kb >= 0, mx > EXP_ZERO_BELOW)

    def body(st):
        kb, _ = st
        rows = pl.ds(pl.multiple_of(kb * tk, tk), tk)
        pvs, carries = _sb_blocks(qbs, [kp_ref[rows, cols(h)] for h in heads], [vp_ref[rows, cols(h)] for h in heads],
                                  [carry_ref[h] for h in heads], None, mtri)
        for h in heads:
            acc_ref[:, cols(h)] += pvs[h]
            carry_ref[h] = carries[h]
        return kb - 1, jnp.max(functools.reduce(jnp.maximum, carries))

    lax.while_loop(cond, body, (n_past - 1, jnp.max(functools.reduce(jnp.maximum, carries))))
    for h in heads:
        o = acc_ref[:, cols(h)]
        o_ref[:, cols(h)] = o * lax.rsqrt(jnp.mean(o * o, axis=-1, keepdims=True) + EPS) * g


def _sb_prompt_kernel(q_ref, k_ref, v_ref, g_ref, o_ref, acc_ref, carry_ref):
    tq = q_ref.shape[0]
    qi = pl.program_id(1)
    _sb_attend(q_ref, k_ref, v_ref, pl.multiple_of(qi * tq, tq), k_ref, v_ref, qi, tq, g_ref[...],
               o_ref, acc_ref, carry_ref)


def _sb_scratch(tq):
    return [pltpu.VMEM((tq, SB_W), F32), pltpu.VMEM((SB_HEADS, tq, 1), F32)]


def _sb_prompt(rest, kb, vb, g_sb, batch, seq, tq=128):
    nq = seq // tq
    whole = pl.BlockSpec((seq, SB_W), lambda b, i: (b, 0), pipeline_mode=pl.Buffered(1))
    return pl.pallas_call(
        _sb_prompt_kernel,
        out_shape=jax.ShapeDtypeStruct((batch * seq, SB_W), F32),
        grid=(batch, nq),
        in_specs=[pl.BlockSpec((tq, SB_W), lambda b, i: (b * nq + i, COL_QSB)), whole, whole, _full(g_sb)],
        out_specs=pl.BlockSpec((tq, SB_W), lambda b, i: (b * nq + i, 0)),
        scratch_shapes=_sb_scratch(tq),
        compiler_params=_params("parallel", "arbitrary"),
        name="sb_prompt",
    )(rest, kb, vb, g_sb)


def _sb_sample_kernel(q_ref, kn_ref, vn_ref, kp_ref, vp_ref, g_ref, o_ref, acc_ref, carry_ref, *, tk):
    _sb_attend(q_ref, kn_ref, vn_ref, 0, kp_ref, vp_ref, kp_ref.shape[0] // tk, tk, g_ref[...],
               o_ref, acc_ref, carry_ref)


def _sb_sample(rest, kb, vb, past_k, past_v, g_sb, batch, seq, tk=128):
    past = past_k.shape[1]
    new = pl.BlockSpec((seq, SB_W), lambda b: (b, 0))
    old = pl.BlockSpec((None, past, SB_W), lambda b: (b, 0, 0))
    return pl.pallas_call(
        functools.partial(_sb_sample_kernel, tk=tk),
        out_shape=jax.ShapeDtypeStruct((batch * seq, SB_W), F32),
        grid=(batch,),
        in_specs=[pl.BlockSpec((seq, SB_W), lambda b: (b, COL_QSB)), new, new, old, old, _full(g_sb)],
        out_specs=new,
        scratch_shapes=_sb_scratch(seq),
        compiler_params=_params("parallel"),
        name="sb_sample",
    )(rest, kb, vb, past_k, past_v, g_sb)


def _gla_kernel(q_ref, k_ref, v_ref, r_ref, lr_ref, wlr_ref, blr_ref, gn_ref, s0_ref, o_ref, sout_ref,
                st_ref, sj_ref, *, c):
    si = pl.program_id(2)
    rows = q_ref.shape[0]
    n_chunks = rows // c
    shift = c.bit_length() - 1
    assert c == 1 << shift

    @pl.when(si == 0)
    def _():
        st_ref[...] = s0_ref[...].T

    causal = _tri((rows, rows), lambda r, cc: jnp.logical_and((r >> shift) == (cc >> shift), cc <= r))
    ltri = jnp.where(causal, 1.0, 0.0).astype(BF16)
    pre = jnp.dot(lr_ref[...].astype(BF16), wlr_ref[...], preferred_element_type=F32) + blr_ref[...]
    la = (jnp.minimum(pre, 0.0) - jnp.log1p(jnp.exp(-jnp.abs(pre)))) * (1.0 / GATE_TAU)
    hi, mid, lo = _split3(la)
    dot = lambda a: jnp.dot(ltri, a, preferred_element_type=F32)
    b = (dot(lo) + dot(mid)) + dot(hi)
    k = k_ref[...]
    vb = v_ref[...].astype(BF16)
    qe = (q_ref[...] * (GLA_DK ** -0.5) * jnp.exp(b)).astype(BF16)
    ke = (k * jnp.exp(-b)).astype(BF16)
    att = lax.dot_general(qe, ke, _NT, preferred_element_type=F32)
    att = jnp.where(causal, att, 0.0).astype(BF16)
    o_intra = jnp.dot(att, vb, preferred_element_type=F32)

    st = st_ref[...]
    for j in range(n_chunks):
        sl = slice(j * c, (j + 1) * c)
        sj_ref[j] = st.astype(BF16)
        b_last = b[(j + 1) * c - 1:(j + 1) * c, :]
        kd = (k[sl, :] * jnp.exp(b_last - b[sl, :])).astype(BF16)
        st = st * jnp.exp(b_last) + lax.dot_general(vb[sl, :], kd, _TN, preferred_element_type=F32)
    st_ref[...] = st

    for j in range(n_chunks):
        sl = slice(j * c, (j + 1) * c)
        o = o_intra[sl, :] + lax.dot_general(qe[sl, :], sj_ref[j], _NT, preferred_element_type=F32)
        on = o * lax.rsqrt(jnp.mean(o * o, axis=-1, keepdims=True) + EPS) * gn_ref[...]
        r = r_ref[sl, :]
        o_ref[sl, :] = on * (r * (1.0 / (1.0 + jnp.exp(-r))))

    @pl.when(si == pl.num_programs(2) - 1)
    def _():
        sout_ref[...] = st_ref[...].T


def _gla(rest, lr, wlr, blr, g_gla, s0, batch, seq):
    c = CHUNK if seq % CHUNK == 0 else seq
    rows = min(seq, 8 * c)
    ns = seq // rows
    row = lambda b, h, s: b * ns + s
    return pl.pallas_call(
        functools.partial(_gla_kernel, c=c),
        out_shape=(jax.ShapeDtypeStruct((batch * seq, GLA_V_W), F32),
                   jax.ShapeDtypeStruct((batch, GLA_HEADS, GLA_DK, GLA_DV), F32)),
        grid=(batch, GLA_HEADS, ns),
        in_specs=[
            pl.BlockSpec((rows, GLA_DK), lambda b, h, s: (row(b, h, s), COL_QG + h)),
            pl.BlockSpec((rows, GLA_DK), lambda b, h, s: (row(b, h, s), COL_KG + h)),
            pl.BlockSpec((rows, GLA_DV), lambda b, h, s: (row(b, h, s), COL_VG // 2 + h)),
            pl.BlockSpec((rows, GLA_DV), lambda b, h, s: (row(b, h, s), COL_RG // 2 + h)),
            pl.BlockSpec((rows, LANE), lambda b, h, s: (row(b, h, s), 0)),
            pl.BlockSpec((LANE, GLA_DK), lambda b, h, s: (0, h)),
            pl.BlockSpec((1, GLA_DK), lambda b, h, s: (0, h)),
            pl.BlockSpec((1, GLA_DV), lambda b, h, s: (0, 0)),
            pl.BlockSpec((None, None, GLA_DK, GLA_DV), lambda b, h, s: (b, h, 0, 0)),
        ],
        out_specs=(pl.BlockSpec((rows, GLA_DV), lambda b, h, s: (row(b, h, s), h)),
                   pl.BlockSpec((None, None, GLA_DK, GLA_DV), lambda b, h, s: (b, h, 0, 0))),
        scratch_shapes=[pltpu.VMEM((GLA_DV, GLA_DK), F32), pltpu.VMEM((rows // c, GLA_DV, GLA_DK), BF16)],
        compiler_params=_params("parallel", "parallel", "arbitrary"),
        name="gla",
    )(rest, rest, rest, rest, lr, wlr, blr, g_gla, s0)


def _layernorm(y, g, b):
    mu = jnp.mean(y, axis=-1, keepdims=True)
    d = y - mu
    var = jnp.mean(d * d, axis=-1, keepdims=True)
    return d * lax.rsqrt(var + EPS) * g + b


def _outproj_kernel(osb_ref, og_ref, x_ref, w_ref, g_ref, b_ref, h_ref, hb_ref, *, alpha):
    acc = jnp.dot(osb_ref[...].astype(BF16), w_ref[0:SB_W, :], preferred_element_type=F32)
    acc = acc + jnp.dot(og_ref[...].astype(BF16), w_ref[SB_W:SB_W + GLA_V_W, :], preferred_element_type=F32)
    h = _layernorm(alpha * x_ref[...] + acc, g_ref[...], b_ref[...])
    h_ref[...] = h
    hb_ref[...] = h.astype(BF16)


def _outproj(o_sb, o_g, x, w_out, ln_g, ln_b, alpha, tm=256):
    t, d = x.shape
    tm = min(tm, t)
    tok = lambda w: pl.BlockSpec((tm, w), lambda i: (i, 0))
    return pl.pallas_call(
        functools.partial(_outproj_kernel, alpha=alpha),
        out_shape=(jax.ShapeDtypeStruct((t, d), F32), jax.ShapeDtypeStruct((t, d), BF16)),
        grid=(t // tm,),
        in_specs=[tok(SB_W), tok(GLA_V_W), tok(d), _full(w_out), _full(ln_g), _full(ln_b)],
        out_specs=(tok(d), tok(d)),
        compiler_params=_params("parallel"),
        name="out_proj_ln1",
    )(o_sb, o_g, x, w_out, ln_g, ln_b)


def _top16(s, vals_ref, exact):
    iota = lax.broadcasted_iota(jnp.int32, s.shape, 0).astype(F32) if exact else None
    for r in range(TOPK):
        m = jnp.max(s, axis=0, keepdims=True)
        sel = s == m
        if exact:
            first = jnp.min(jnp.where(sel, iota, float(s.shape[0])), axis=0, keepdims=True)
            sel = iota == first
        s = jnp.where(sel, CODE_BASE + (r + 1) * CODE_STEP, s)
        vals_ref[r:r + 1, :] = m
    coded = s < CODE_BELOW
    rank = jnp.where(coded, (s - CODE_BASE) * (1.0 / CODE_STEP), float(TOPK + 1))
    count = jnp.sum(jnp.where(coded, 1.0, 0.0), axis=0, keepdims=True)
    return rank, count


def _peer_select(s1, s2, a_ref, b_ref, c_ref, best_ref, rv_ref, rk1_ref, rk2_ref, exact):
    rank1, n1 = _top16(s1, a_ref, exact)
    rank2, n2 = _top16(s2, b_ref, exact)
    rk1_ref[...] = rank1
    rk2_ref[...] = rank2
    c_ref[0:8, :] = a_ref[0:1, :] + b_ref[0:8, :]
    c_ref[8:16, :] = a_ref[0:1, :] + b_ref[8:16, :]
    c_ref[16:24, :] = a_ref[1:2, :] + b_ref[0:8, :]
    c_ref[24:32, :] = a_ref[2:3, :] + b_ref[0:8, :]
    c_ref[32:40, :] = a_ref[3:4, :] + b_ref[0:8, :]
    c_ref[40:44, :] = a_ref[4:5, :] + b_ref[0:4, :]
    c_ref[44:48, :] = a_ref[5:6, :] + b_ref[0:4, :]
    c_ref[48:52, :] = a_ref[6:7, :] + b_ref[0:4, :]
    c_ref[52:56, :] = a_ref[7:8, :] + b_ref[0:4, :]
    c_ref[56:64, :] = a_ref[8:16, :] + b_ref[0:1, :]
    crank, n3 = _top16(c_ref[...], best_ref, exact)
    c_ref[...] = jnp.where(crank <= float(TOPK), 1.0, 0.0)
    rsum = lambda lo, hi: jnp.sum(c_ref[lo:hi, :], axis=0, keepdims=True)
    rv_ref[0:1, :] = rsum(0, 16)
    rv_ref[1:2, :] = rsum(16, 24)
    rv_ref[2:3, :] = rsum(24, 32)
    rv_ref[3:4, :] = rsum(32, 40)
    rv_ref[4:5, :] = rsum(40, 44)
    rv_ref[5:6, :] = rsum(44, 48)
    rv_ref[6:7, :] = rsum(48, 52)
    rv_ref[7:8, :] = rsum(52, 56)
    rv_ref[8:16, :] = c_ref[56:64, :]
    return jnp.maximum(jnp.maximum(n1, n2), n3)


def _peer_topk_kernel(hb_ref, wpq_ref, keys_ref, rank2_ref, rfull_ref, e1_ref, e2_ref,
                      s_ref, a_ref, b_ref, c_ref, best_ref, rv_ref, rk1_ref, rk2_ref):
    tm = hb_ref.shape[0]
    q = jnp.dot(hb_ref[...], wpq_ref[...], preferred_element_type=F32).astype(BF16)
    for hp in range(2 * PEER_HEADS):
        s_ref[hp] = lax.dot_general(keys_ref[hp], q[:, hp * PEER_HALF:(hp + 1) * PEER_HALF], _NT,
                                    preferred_element_type=F32)

    groups = range(tm // LANE)
    lanes = [pl.ds(g * LANE, LANE) for g in groups]
    scratch = lambda g: [ref.at[g] for ref in (a_ref, b_ref, c_ref, best_ref, rv_ref, rk1_ref, rk2_ref)]

    def head(h, carry):
        s1 = [s_ref[2 * h, :, lanes[g]] for g in groups]
        s2 = [s_ref[2 * h + 1, :, lanes[g]] for g in groups]
        counts = [_peer_select(s1[g], s2[g], *scratch(g), exact=False) for g in groups]

        @pl.when(jnp.max(functools.reduce(jnp.maximum, counts)) > TOPK + 0.5)
        def _():
            for g in groups:
                _peer_select(s1[g], s2[g], *scratch(g), exact=True)

        for g in groups:
            rank1 = rk1_ref[g]
            rfull = jnp.zeros((N_KEYS, LANE), F32)
            for r in range(TOPK):
                rfull = jnp.where(rank1 == float(r + 1), rv_ref[g, r:r + 1, :], rfull)
            best = best_ref[g]
            zsum = jnp.sum(jnp.exp(best - best[0:1, :]), axis=0, keepdims=True)
            rank2_ref[h, :, lanes[g]] = rk2_ref[g].astype(BF16)
            rfull_ref[h, :, lanes[g]] = rfull
            e1_ref[h, :, lanes[g]] = jnp.exp(s1[g] - a_ref[g, 0:1, :]) / zsum
            e2_ref[h, :, lanes[g]] = jnp.exp(s2[g] - b_ref[g, 0:1, :]).astype(BF16)
        return carry

    lax.fori_loop(0, PEER_HEADS, head, 0)


def _peer_topk(hb, w_pq, keys, tm=512):
    t, d = hb.shape
    tm = min(tm, t)
    ng = tm // LANE
    sel = lambda dt: jax.ShapeDtypeStruct((PEER_HEADS, N_KEYS, t), dt)
    sel_spec = pl.BlockSpec((PEER_HEADS, N_KEYS, tm), lambda i: (0, 0, i))
    row16 = pltpu.VMEM((ng, TOPK, LANE), F32)
    keyrows = pltpu.VMEM((ng, N_KEYS, LANE), F32)
    return pl.pallas_call(
        _peer_topk_kernel,
        out_shape=(sel(BF16), sel(F32), sel(F32), sel(BF16)),
        grid=(t // tm,),
        in_specs=[pl.BlockSpec((tm, d), lambda i: (i, 0)), _full(w_pq), _full(keys)],
        out_specs=(sel_spec, sel_spec, sel_spec, sel_spec),
        scratch_shapes=[pltpu.VMEM((2 * PEER_HEADS, N_KEYS, tm), F32), row16, row16,
                        pltpu.VMEM((ng, 64, LANE), F32), row16, row16, keyrows, keyrows],
        compiler_params=_params("parallel"),
        name="peer_topk",
    )(hb, w_pq, keys)


def _peer_dense_kernel(u_ref, vt_ref, hb_ref, h_ref, rank2_ref, rfull_ref, e1_ref, e2_ref, g_ref, b_ref,
                       o_ref, yt_ref, at_ref, w_ref, r2s_ref, e2s_ref, *, alpha):
    e = pl.program_id(1)
    te, tm = at_ref.shape
    ni = te // N_KEYS
    hw = min(MXU_N, tm)
    halves = [pl.ds(c * hw, hw) for c in range(tm // hw)]

    @pl.when(e == 0)
    def _():
        yt_ref[...] = jnp.zeros_like(yt_ref)
        r2s_ref[...] = rank2_ref[...]
        e2s_ref[...] = e2_ref[...]

    for hl in halves:
        at_ref[:, hl] = lax.dot_general(u_ref[...], hb_ref[hl, :], _NT, preferred_element_type=F32)

    igroup = pl.ds(pl.multiple_of(e * ni, SUBLANE), ni)
    for c, hl in enumerate(halves):
        for lg in range(hw // LANE):
            lanes = pl.ds(c * hw + lg * LANE, LANE)
            for il in range(ni):
                rows = pl.ds(il * N_KEYS, N_KEYS)
                gate = None
                for h in range(PEER_HEADS):
                    bcast = lambda ref: jnp.broadcast_to(ref[h, igroup, lanes][il:il + 1, :].astype(BF16), (N_KEYS, LANE))
                    e2 = e2s_ref[h, :, lanes]
                    term = jnp.where(r2s_ref[h, :, lanes] <= bcast(rfull_ref), e2 * bcast(e1_ref), jnp.zeros_like(e2))
                    gate = term if gate is None else gate + term
                act = jax.nn.gelu(at_ref[rows, lanes], approximate=True).astype(BF16)
                w_ref[rows, lanes] = act * gate
        yt_ref[:, hl] += jnp.dot(vt_ref[...], w_ref[:, hl], preferred_element_type=F32)

    @pl.when(e == pl.num_programs(1) - 1)
    def _():
        o_ref[...] = _layernorm(alpha * h_ref[...] + yt_ref[...].T, g_ref[...], b_ref[...])


def _peer_dense(u, vt, hb, h, sel, ln_g, ln_b, alpha, tm=512, te=SUBLANE * N_KEYS):
    t, d = h.shape
    ne = u.shape[0]
    tm = min(tm, t)
    once = pl.Buffered(1)
    sel_spec = pl.BlockSpec((PEER_HEADS, N_KEYS, tm), lambda i, e: (0, 0, i), pipeline_mode=once)
    tok_in = pl.BlockSpec((tm, d), lambda i, e: (i, 0), pipeline_mode=once)
    tok = pl.BlockSpec((tm, d), lambda i, e: (i, 0))
    vec = pl.BlockSpec((1, d), lambda i, e: (0, 0))
    return pl.pallas_call(
        functools.partial(_peer_dense_kernel, alpha=alpha),
        out_shape=jax.ShapeDtypeStruct((t, d), F32),
        grid=(t // tm, ne // te),
        in_specs=[pl.BlockSpec((te, d), lambda i, e: (e, 0)), pl.BlockSpec((d, te), lambda i, e: (0, e)),
                  tok_in, tok_in, sel_spec, sel_spec, sel_spec, sel_spec, vec, vec],
        out_specs=tok,
        scratch_shapes=[pltpu.VMEM((d, tm), F32), pltpu.VMEM((te, tm), F32), pltpu.VMEM((te, tm), BF16),
                        pltpu.VMEM((PEER_HEADS, N_KEYS, tm), BF16), pltpu.VMEM((PEER_HEADS, N_KEYS, tm), BF16)],
        compiler_params=_params("parallel", "arbitrary"),
        name="peer_dense_ln2",
    )(u, vt, hb, h, *sel, ln_g, ln_b)


def _ple_kernel(h_ref, pe_ref, wg_ref, wp_ref, o_ref):
    h = h_ref[...]
    gate = jnp.dot(h.astype(BF16), wg_ref[...], preferred_element_type=F32)
    emb = jnp.dot(pe_ref[...].astype(BF16), wp_ref[...], preferred_element_type=F32)
    o_ref[...] = h + (1.0 / (1.0 + jnp.exp(-gate))) * emb


def _ple(h, pe, w_gate, w_ple, tm=256):
    t, d = h.shape
    tm = min(tm, t)
    p = pe.shape[1]
    return pl.pallas_call(
        _ple_kernel,
        out_shape=jax.ShapeDtypeStruct((t, d), F32),
        grid=(t // tm,),
        in_specs=[pl.BlockSpec((tm, d), lambda i: (i, 0)), pl.BlockSpec((tm, p), lambda i: (i, 0)),
                  _full(w_gate), _full(w_ple)],
        out_specs=pl.BlockSpec((tm, d), lambda i: (i, 0)),
        compiler_params=_params("parallel"),
        name="ple_gate",
    )(h, pe, w_gate, w_ple)


def _layer(x, pe, past_k, past_v, s0, wts, alpha):
    batch, seq, d = x.shape
    t = batch * seq
    x2 = x.reshape(t, d)
    k_sb, v_sb, kb, vb, rest, lr = _in_proj(x2, wts["w_in"], wts["w_in_lr"])
    if past_k is None:
        o_sb = _sb_prompt(rest, kb, vb, wts["g_sb"], batch, seq)
    else:
        past = past_k.shape[1]
        o_sb = _sb_sample(rest, kb, vb, past_k.reshape(batch, past, SB_W), past_v.reshape(batch, past, SB_W),
                          wts["g_sb"], batch, seq)
    o_g, s_new = _gla(rest, lr, wts["w_lr"], wts["b_lr"], wts["g_gla"], s0, batch, seq)
    h1, h1b = _outproj(o_sb, o_g, x2, wts["w_out"], wts["ln1_g"], wts["ln1_b"], alpha)
    sel = _peer_topk(h1b, wts["w_pq"], wts["keys"])
    h2 = _peer_dense(wts["u"], wts["vt"], h1b, h1, sel, wts["ln2_g"], wts["ln2_b"], alpha)
    out = _ple(h2, pe.reshape(t, -1), wts["w_gate"], wts["w_ple"])
    heads = lambda a: a.reshape(batch, seq, SB_HEADS, SB_D)
    return out.reshape(batch, seq, d), heads(k_sb), heads(v_sb), s_new


def _prep_weights(w_in, w_gla_lr, b_gla_lr, g_sb_norm, g_gla_norm, w_out, ln1_g, ln1_b,
                  w_pq, peer_keys, expert_u, expert_v, ln2_g, ln2_b, w_ple, w_ple_gate):
    d = w_in.shape[0]
    lr0 = 3 * SB_W + 2 * GLA_K_W + GLA_V_W
    w_main = jnp.concatenate([w_in[:, SB_W:3 * SB_W], w_in[:, :SB_W], w_in[:, 3 * SB_W:lr0],
                              w_in[:, lr0 + GLA_LR:]], axis=1).astype(BF16)
    w_in_lr = jnp.concatenate([w_in[:, lr0:lr0 + GLA_LR], jnp.zeros((d, LANE - GLA_LR), w_in.dtype)], axis=1).astype(BF16)
    w_lr = jnp.concatenate([w_gla_lr, jnp.zeros((LANE - GLA_LR, GLA_K_W), w_gla_lr.dtype)], axis=0).astype(BF16)
    row = lambda a: a.reshape(1, -1)
    return {
        "w_in": w_main, "w_in_lr": w_in_lr, "w_lr": w_lr, "b_lr": row(b_gla_lr),
        "g_sb": row(g_sb_norm), "g_gla": row(g_gla_norm),
        "w_out": w_out.astype(BF16), "ln1_g": row(ln1_g), "ln1_b": row(ln1_b),
        "w_pq": w_pq.astype(BF16), "keys": peer_keys.reshape(2 * PEER_HEADS, N_KEYS, PEER_HALF).astype(BF16),
        "u": expert_u.astype(BF16), "vt": expert_v.astype(BF16).T,
        "ln2_g": row(ln2_g), "ln2_b": row(ln2_b), "w_ple": w_ple.astype(BF16), "w_gate": w_ple_gate.astype(BF16),
    }


def kernel(x_prompt, x_sample, cache_sb_k, cache_sb_v, state_gla, p_prompt, p_sample, w_in, w_gla_lr, b_gla_lr, g_sb_norm, g_gla_norm, w_out, ln1_g, ln1_b, w_pq, peer_keys, expert_u, expert_v, ln2_g, ln2_b, w_ple, w_ple_gate):
    depth = w_in.shape[0]
    alpha = (2.0 * depth) ** 0.25
    hp, hs = x_prompt, x_sample
    outs = [[] for _ in range(6)]
    for i in range(depth):
        wts = _prep_weights(w_in[i], w_gla_lr[i], b_gla_lr[i], g_sb_norm[i], g_gla_norm[i], w_out[i],
                            ln1_g[i], ln1_b[i], w_pq[i], peer_keys[i], expert_u[i], expert_v[i],
                            ln2_g[i], ln2_b[i], w_ple[i], w_ple_gate[i])
        s0p = jnp.zeros((hp.shape[0], GLA_HEADS, GLA_DK, GLA_DV), F32)
        hp, kp, vp, sp = _layer(hp, p_prompt[i], None, None, s0p, wts, alpha)
        hs, ks, vs, ss = _layer(hs, p_sample[i], cache_sb_k[i], cache_sb_v[i], state_gla[i], wts, alpha)
        for lst, val in zip(outs, (kp, vp, sp, ks, vs, ss)):
            lst.append(val)
    return (hp, hs) + tuple(jnp.stack(lst) for lst in outs)
```

```python
import functools

import jax
import jax.numpy as jnp
from jax import lax
from jax.experimental import pallas as pl
from jax.experimental.pallas import tpu as pltpu

BF16 = jnp.bfloat16
F32 = jnp.float32

SB_HEADS = 8
SB_D = 128
SB_W = SB_HEADS * SB_D
GLA_HEADS = 4
GLA_DK = 128
GLA_DV = 256
GLA_K_W = GLA_HEADS * GLA_DK
GLA_V_W = GLA_HEADS * GLA_DV
GLA_LR = 16
GATE_TAU = 16.0
CHUNK = 64
PEER_HEADS = 8
PEER_HALF = 128
N_KEYS = 128
TOPK = 16
EPS = 1e-6

LANE = 128
SUBLANE = 8
BF16_ROWS = 16
MXU_N = 256
VMEM_LIMIT = 56 * 1024 * 1024

COL_QSB = 0
COL_QG = SB_W // LANE
COL_KG = COL_QG + GLA_K_W // LANE
COL_VG = COL_KG + GLA_K_W // LANE
COL_RG = COL_VG + GLA_V_W // LANE
REST_W = SB_W + 2 * GLA_K_W + 2 * GLA_V_W
PROJ_TILE = 1024

PEER_TE = SUBLANE * N_KEYS
DMA_SPLIT = 4

EXP_ZERO_BELOW = -88.0

CODE_BASE = -(2.0 ** 127)
CODE_STEP = 2.0 ** 110
CODE_BELOW = -(2.0 ** 126)

_NT = (((1,), (1,)), ((), ()))
_TN = (((0,), (0,)), ((), ()))


def _params(*sem):
    return pltpu.CompilerParams(dimension_semantics=sem, vmem_limit_bytes=VMEM_LIMIT)


def _full(a):
    return pl.BlockSpec(a.shape, lambda *_: (0,) * a.ndim)


def _in_proj_kernel(x_ref, w_ref, wlr_ref, k_ref, v_ref, kb_ref, vb_ref, rest_ref, lr_ref, xb_ref):
    j = pl.program_id(1)
    dot = lambda w: jnp.dot(xb_ref[...], w[...], preferred_element_type=F32)

    @pl.when(j == 0)
    def _():
        xb_ref[...] = x_ref[...].astype(BF16)
        lr_ref[...] = dot(wlr_ref)
        acc = dot(w_ref)
        k_ref[...] = acc
        kb_ref[...] = acc.astype(BF16)

    @pl.when(j == 1)
    def _():
        acc = dot(w_ref)
        v_ref[...] = acc
        vb_ref[...] = acc.astype(BF16)

    @pl.when(j >= 2)
    def _():
        rest_ref[...] = dot(w_ref)


def _in_proj(x, w_main, w_lr, tm=512):
    t, d = x.shape
    tm = min(tm, t)
    n_tiles = w_main.shape[0]
    tok = lambda w: pl.BlockSpec((tm, w), lambda i, j: (i, 0))
    sds = lambda w, dt: jax.ShapeDtypeStruct((t, w), dt)
    return pl.pallas_call(
        _in_proj_kernel,
        out_shape=(sds(SB_W, F32), sds(SB_W, F32), sds(SB_W, BF16), sds(SB_W, BF16), sds(REST_W, F32), sds(LANE, F32)),
        grid=(t // tm, n_tiles),
        in_specs=[tok(d), pl.BlockSpec((None, d, PROJ_TILE), lambda i, j: (j, 0, 0)), pl.BlockSpec((d, LANE), lambda i, j: (0, 0))],
        out_specs=(tok(SB_W), tok(SB_W), tok(SB_W), tok(SB_W),
                   pl.BlockSpec((tm, PROJ_TILE), lambda i, j: (i, jnp.maximum(j - 2, 0))), tok(LANE)),
        scratch_shapes=[pltpu.VMEM((tm, d), BF16)],
        compiler_params=_params("parallel", "arbitrary"),
        name="in_proj",
    )(x, w_main, w_lr)


def _logsig_pair(z):
    l1p = jnp.log1p(jnp.exp(-jnp.abs(z)))
    return jnp.minimum(z, 0.0) - l1p, -jnp.maximum(z, 0.0) - l1p


def _split3(x):
    hi = x.astype(BF16)
    r = x - hi.astype(F32)
    mid = r.astype(BF16)
    lo = (r - mid.astype(F32)).astype(BF16)
    return hi, mid, lo


def _tri(shape, fn):
    row = lax.broadcasted_iota(jnp.int32, shape, 0)
    col = lax.broadcasted_iota(jnp.int32, shape, 1)
    return fn(row, col)


def _sb_blocks(qbs, ks, vs, carries, mask, mtri):
    zs = [lax.dot_general(q, k.astype(BF16), _NT, preferred_element_type=F32) * (SB_D ** -0.5)
          for q, k in zip(qbs, ks)]
    pairs = [_logsig_pair(z) for z in zs]
    lfs = [lf if mask is None else jnp.where(mask, lf, 0.0) for _, lf in pairs]
    splits = [_split3(lf) for lf in lfs]
    dot = lambda a: jnp.dot(a, mtri, preferred_element_type=F32)
    cums = [(dot(lo) + dot(mid)) + dot(hi) for hi, mid, lo in splits]
    ws = [jnp.exp(ls + (c + cum)) for (ls, _), c, cum in zip(pairs, carries, cums)]
    if mask is not None:
        ws = [jnp.where(mask, w, 0.0) for w in ws]
    pvs = [jnp.dot(w.astype(BF16), v.astype(BF16), preferred_element_type=F32) for w, v in zip(ws, vs)]
    return pvs, [c + jnp.sum(lf, axis=1, keepdims=True) for c, lf in zip(carries, lfs)]


def _sb_attend(q_ref, kd_ref, vd_ref, d0, kp_ref, vp_ref, n_past, tk, g, o_ref, acc_ref, carry_ref):
    tq = q_ref.shape[0]
    heads = range(SB_HEADS)
    cols = lambda h: slice(h * SB_D, (h + 1) * SB_D)
    qbs = [q_ref[:, cols(h)].astype(BF16) for h in heads]
    mask = _tri((tq, tq), lambda r, c: c < r)
    mtri_d = _tri((tq, tq), lambda r, c: jnp.where(r > c, 1.0, 0.0)).astype(BF16)
    drows = pl.ds(d0, tq)
    pvs, carries = _sb_blocks(qbs, [kd_ref[drows, cols(h)] for h in heads], [vd_ref[drows, cols(h)] for h in heads],
                              [jnp.zeros((tq, 1), F32)] * SB_HEADS, mask, mtri_d)
    for h in heads:
        acc_ref[:, cols(h)] = pvs[h]
        carry_ref[h] = carries[h]
    mtri = _tri((tk, tk), lambda r, c: jnp.where(r > c, 1.0, 0.0)).astype(BF16)

    def cond(st):
        kb, mx = st
        return jnp.logical_and(kb >= 0, mx > EXP_ZERO_BELOW)

    def body(st):
        kb, _ = st
        rows = pl.ds(pl.multiple_of(kb * tk, tk), tk)
        pvs, carries = _sb_blocks(qbs, [kp_ref[rows, cols(h)] for h in heads], [vp_ref[rows, cols(h)] for h in heads],
                                  [carry_ref[h] for h in heads], None, mtri)
        for h in heads:
            acc_ref[:, cols(h)] += pvs[h]
            carry_ref[h] = carries[h]
        return kb - 1, jnp.max(functools.reduce(jnp.maximum, carries))

    lax.while_loop(cond, body, (n_past - 1, jnp.max(functools.reduce(jnp.maximum, carries))))
    for h in heads:
        o = acc_ref[:, cols(h)]
        o_ref[:, cols(h)] = o * lax.rsqrt(jnp.mean(o * o, axis=-1, keepdims=True) + EPS) * g


def _sb_prompt_kernel(q_ref, k_ref, v_ref, g_ref, o_ref, acc_ref, carry_ref):
    tq = q_ref.shape[0]
    qi = pl.program_id(1)
    _sb_attend(q_ref, k_ref, v_ref, pl.multiple_of(qi * tq, tq), k_ref, v_ref, qi, tq, g_ref[...],
               o_ref, acc_ref, carry_ref)


def _sb_scratch(tq):
    return [pltpu.VMEM((tq, SB_W), F32), pltpu.VMEM((SB_HEADS, tq, 1), F32)]


def _sb_prompt(rest, kb, vb, g_sb, batch, seq, tq=128):
    nq = seq // tq
    whole = pl.BlockSpec((seq, SB_W), lambda b, i: (b, 0), pipeline_mode=pl.Buffered(1))
    return pl.pallas_call(
        _sb_prompt_kernel,
        out_shape=jax.ShapeDtypeStruct((batch * seq, SB_W), F32),
        grid=(batch, nq),
        in_specs=[pl.BlockSpec((tq, SB_W), lambda b, i: (b * nq + i, COL_QSB)), whole, whole, _full(g_sb)],
        out_specs=pl.BlockSpec((tq, SB_W), lambda b, i: (b * nq + i, 0)),
        scratch_shapes=_sb_scratch(tq),
        compiler_params=_params("parallel", "arbitrary"),
        name="sb_prompt",
    )(rest, kb, vb, g_sb)


def _sb_sample_kernel(q_ref, kn_ref, vn_ref, kp_ref, vp_ref, g_ref, o_ref, acc_ref, carry_ref, *, tk):
    _sb_attend(q_ref, kn_ref, vn_ref, 0, kp_ref, vp_ref, kp_ref.shape[0] // tk, tk, g_ref[...],
               o_ref, acc_ref, carry_ref)


def _sb_sample(rest, kb, vb, past_k, past_v, g_sb, batch, seq, tk=128):
    past = past_k.shape[1]
    new = pl.BlockSpec((seq, SB_W), lambda b: (b, 0))
    old = pl.BlockSpec((None, past, SB_W), lambda b: (b, 0, 0))
    return pl.pallas_call(
        functools.partial(_sb_sample_kernel, tk=tk),
        out_shape=jax.ShapeDtypeStruct((batch * seq, SB_W), F32),
        grid=(batch,),
        in_specs=[pl.BlockSpec((seq, SB_W), lambda b: (b, COL_QSB)), new, new, old, old, _full(g_sb)],
        out_specs=new,
        scratch_shapes=_sb_scratch(seq),
        compiler_params=_params("parallel"),
        name="sb_sample",
    )(rest, kb, vb, past_k, past_v, g_sb)


def _gla_kernel(q_ref, k_ref, v_ref, r_ref, lr_ref, wlr_ref, blr_ref, gn_ref, s0_ref, o_ref, sout_ref,
                st_ref, sj_ref, *, c):
    si = pl.program_id(2)
    rows = q_ref.shape[0]
    n_chunks = rows // c
    shift = c.bit_length() - 1
    assert c == 1 << shift

    @pl.when(si == 0)
    def _():
        st_ref[...] = s0_ref[...].T

    causal = _tri((rows, rows), lambda r, cc: jnp.logical_and((r >> shift) == (cc >> shift), cc <= r))
    ltri = jnp.where(causal, 1.0, 0.0).astype(BF16)
    pre = jnp.dot(lr_ref[...].astype(BF16), wlr_ref[...], preferred_element_type=F32) + blr_ref[...]
    la = (jnp.minimum(pre, 0.0) - jnp.log1p(jnp.exp(-jnp.abs(pre)))) * (1.0 / GATE_TAU)
    hi, mid, lo = _split3(la)
    dot = lambda a: jnp.dot(ltri, a, preferred_element_type=F32)
    b = (dot(lo) + dot(mid)) + dot(hi)
    k = k_ref[...]
    vb = v_ref[...].astype(BF16)
    qe = (q_ref[...] * (GLA_DK ** -0.5) * jnp.exp(b)).astype(BF16)
    ke = (k * jnp.exp(-b)).astype(BF16)
    att = lax.dot_general(qe, ke, _NT, preferred_element_type=F32)
    att = jnp.where(causal, att, 0.0).astype(BF16)
    o_intra = jnp.dot(att, vb, preferred_element_type=F32)

    st = st_ref[...]
    for j in range(n_chunks):
        sl = slice(j * c, (j + 1) * c)
        sj_ref[j] = st.astype(BF16)
        b_last = b[(j + 1) * c - 1:(j + 1) * c, :]
        kd = (k[sl, :] * jnp.exp(b_last - b[sl, :])).astype(BF16)
        st = st * jnp.exp(b_last) + lax.dot_general(vb[sl, :], kd, _TN, preferred_element_type=F32)
    st_ref[...] = st

    for j in range(n_chunks):
        sl = slice(j * c, (j + 1) * c)
        o = o_intra[sl, :] + lax.dot_general(qe[sl, :], sj_ref[j], _NT, preferred_element_type=F32)
        on = o * lax.rsqrt(jnp.mean(o * o, axis=-1, keepdims=True) + EPS) * gn_ref[...]
        r = r_ref[sl, :]
        o_ref[sl, :] = on * (r * (1.0 / (1.0 + jnp.exp(-r))))

    @pl.when(si == pl.num_programs(2) - 1)
    def _():
        sout_ref[...] = st_ref[...].T


def _gla(rest, lr, wlr, blr, g_gla, s0, batch, seq):
    c = CHUNK if seq % CHUNK == 0 else seq
    rows = min(seq, 8 * c)
    ns = seq // rows
    row = lambda b, h, s: b * ns + s
    return pl.pallas_call(
        functools.partial(_gla_kernel, c=c),
        out_shape=(jax.ShapeDtypeStruct((batch * seq, GLA_V_W), F32),
                   jax.ShapeDtypeStruct((batch, GLA_HEADS, GLA_DK, GLA_DV), F32)),
        grid=(batch, GLA_HEADS, ns),
        in_specs=[
            pl.BlockSpec((rows, GLA_DK), lambda b, h, s: (row(b, h, s), COL_QG + h)),
            pl.BlockSpec((rows, GLA_DK), lambda b, h, s: (row(b, h, s), COL_KG + h)),
            pl.BlockSpec((rows, GLA_DV), lambda b, h, s: (row(b, h, s), COL_VG // 2 + h)),
            pl.BlockSpec((rows, GLA_DV), lambda b, h, s: (row(b, h, s), COL_RG // 2 + h)),
            pl.BlockSpec((rows, LANE), lambda b, h, s: (row(b, h, s), 0)),
            pl.BlockSpec((LANE, GLA_DK), lambda b, h, s: (0, h)),
            pl.BlockSpec((1, GLA_DK), lambda b, h, s: (0, h)),
            pl.BlockSpec((1, GLA_DV), lambda b, h, s: (0, 0)),
            pl.BlockSpec((None, None, GLA_DK, GLA_DV), lambda b, h, s: (b, h, 0, 0)),
        ],
        out_specs=(pl.BlockSpec((rows, GLA_DV), lambda b, h, s: (row(b, h, s), h)),
                   pl.BlockSpec((None, None, GLA_DK, GLA_DV), lambda b, h, s: (b, h, 0, 0))),
        scratch_shapes=[pltpu.VMEM((GLA_DV, GLA_DK), F32), pltpu.VMEM((rows // c, GLA_DV, GLA_DK), BF16)],
        compiler_params=_params("parallel", "parallel", "arbitrary"),
        name="gla",
    )(rest, rest, rest, rest, lr, wlr, blr, g_gla, s0)


def _layernorm(y, g, b):
    mu = jnp.mean(y, axis=-1, keepdims=True)
    d = y - mu
    var = jnp.mean(d * d, axis=-1, keepdims=True)
    return d * lax.rsqrt(var + EPS) * g + b


def _outproj_kernel(osb_ref, og_ref, x_ref, w_ref, g_ref, b_ref, h_ref, hb_ref, *, alpha):
    acc = jnp.dot(osb_ref[...].astype(BF16), w_ref[0:SB_W, :], preferred_element_type=F32)
    acc = acc + jnp.dot(og_ref[...].astype(BF16), w_ref[SB_W:SB_W + GLA_V_W, :], preferred_element_type=F32)
    h = _layernorm(alpha * x_ref[...] + acc, g_ref[...], b_ref[...])
    h_ref[...] = h
    hb_ref[...] = h.astype(BF16)


def _outproj(o_sb, o_g, x, w_out, ln_g, ln_b, alpha, tm=256):
    t, d = x.shape
    tm = min(tm, t)
    tok = lambda w: pl.BlockSpec((tm, w), lambda i: (i, 0))
    return pl.pallas_call(
        functools.partial(_outproj_kernel, alpha=alpha),
        out_shape=(jax.ShapeDtypeStruct((t, d), F32), jax.ShapeDtypeStruct((t, d), BF16)),
        grid=(t // tm,),
        in_specs=[tok(SB_W), tok(GLA_V_W), tok(d), _full(w_out), _full(ln_g), _full(ln_b)],
        out_specs=(tok(d), tok(d)),
        compiler_params=_params("parallel"),
        name="out_proj_ln1",
    )(o_sb, o_g, x, w_out, ln_g, ln_b)


def _top16(s, vals_ref, exact):
    iota = lax.broadcasted_iota(jnp.int32, s.shape, 0).astype(F32) if exact else None
    for r in range(TOPK):
        m = jnp.max(s, axis=0, keepdims=True)
        sel = s == m
        if exact:
            first = jnp.min(jnp.where(sel, iota, float(s.shape[0])), axis=0, keepdims=True)
            sel = iota == first
        s = jnp.where(sel, CODE_BASE + (r + 1) * CODE_STEP, s)
        vals_ref[r:r + 1, :] = m
    coded = s < CODE_BELOW
    rank = jnp.where(coded, (s - CODE_BASE) * (1.0 / CODE_STEP), float(TOPK + 1))
    count = jnp.sum(jnp.where(coded, 1.0, 0.0), axis=0, keepdims=True)
    return rank, count


def _peer_select(s1, s2, a_ref, b_ref, c_ref, best_ref, rv_ref, rk1_ref, rk2_ref, exact):
    rank1, n1 = _top16(s1, a_ref, exact)
    rank2, n2 = _top16(s2, b_ref, exact)
    rk1_ref[...] = rank1
    rk2_ref[...] = rank2
    c_ref[0:8, :] = a_ref[0:1, :] + b_ref[0:8, :]
    c_ref[8:16, :] = a_ref[0:1, :] + b_ref[8:16, :]
    c_ref[16:24, :] = a_ref[1:2, :] + b_ref[0:8, :]
    c_ref[24:32, :] = a_ref[2:3, :] + b_ref[0:8, :]
    c_ref[32:40, :] = a_ref[3:4, :] + b_ref[0:8, :]
    c_ref[40:44, :] = a_ref[4:5, :] + b_ref[0:4, :]
    c_ref[44:48, :] = a_ref[5:6, :] + b_ref[0:4, :]
    c_ref[48:52, :] = a_ref[6:7, :] + b_ref[0:4, :]
    c_ref[52:56, :] = a_ref[7:8, :] + b_ref[0:4, :]
    c_ref[56:64, :] = a_ref[8:16, :] + b_ref[0:1, :]
    crank, n3 = _top16(c_ref[...], best_ref, exact)
    c_ref[...] = jnp.where(crank <= float(TOPK), 1.0, 0.0)
    rsum = lambda lo, hi: jnp.sum(c_ref[lo:hi, :], axis=0, keepdims=True)
    rv_ref[0:1, :] = rsum(0, 16)
    rv_ref[1:2, :] = rsum(16, 24)
    rv_ref[2:3, :] = rsum(24, 32)
    rv_ref[3:4, :] = rsum(32, 40)
    rv_ref[4:5, :] = rsum(40, 44)
    rv_ref[5:6, :] = rsum(44, 48)
    rv_ref[6:7, :] = rsum(48, 52)
    rv_ref[7:8, :] = rsum(52, 56)
    rv_ref[8:16, :] = c_ref[56:64, :]
    return jnp.maximum(jnp.maximum(n1, n2), n3)


def _peer_topk_kernel(hb_ref, wpq_ref, keys_ref, rank2_ref, rfull_ref, e1_ref, e2_ref,
                      s_ref, a_ref, b_ref, c_ref, best_ref, rv_ref, rk1_ref, rk2_ref):
    tm = hb_ref.shape[0]
    q = jnp.dot(hb_ref[...], wpq_ref[...], preferred_element_type=F32).astype(BF16)
    for hp in range(2 * PEER_HEADS):
        s_ref[hp] = lax.dot_general(keys_ref[hp], q[:, hp * PEER_HALF:(hp + 1) * PEER_HALF], _NT,
                                    preferred_element_type=F32)

    groups = range(tm // LANE)
    lanes = [pl.ds(g * LANE, LANE) for g in groups]
    scratch = lambda g: [ref.at[g] for ref in (a_ref, b_ref, c_ref, best_ref, rv_ref, rk1_ref, rk2_ref)]

    def head(h, carry):
        s1 = [s_ref[2 * h, :, lanes[g]] for g in groups]
        s2 = [s_ref[2 * h + 1, :, lanes[g]] for g in groups]
        counts = [_peer_select(s1[g], s2[g], *scratch(g), exact=False) for g in groups]

        @pl.when(jnp.max(functools.reduce(jnp.maximum, counts)) > TOPK + 0.5)
        def _():
            for g in groups:
                _peer_select(s1[g], s2[g], *scratch(g), exact=True)

        for g in groups:
            rank1 = rk1_ref[g]
            rfull = jnp.zeros((N_KEYS, LANE), F32)
            for r in range(TOPK):
                rfull = jnp.where(rank1 == float(r + 1), rv_ref[g, r:r + 1, :], rfull)
            best = best_ref[g]
            zsum = jnp.sum(jnp.exp(best - best[0:1, :]), axis=0, keepdims=True)
            rank2_ref[h, :, lanes[g]] = rk2_ref[g].astype(BF16)
            rfull_ref[h, :, lanes[g]] = rfull
            e1_ref[h, :, lanes[g]] = jnp.exp(s1[g] - a_ref[g, 0:1, :]) / zsum
            e2_ref[h, :, lanes[g]] = jnp.exp(s2[g] - b_ref[g, 0:1, :]).astype(BF16)
        return carry

    lax.fori_loop(0, PEER_HEADS, head, 0)


def _peer_topk(hb, w_pq, keys, tm=512):
    t, d = hb.shape
    tm = min(tm, t)
    ng = tm // LANE
    sel = lambda dt: jax.ShapeDtypeStruct((PEER_HEADS, N_KEYS, t), dt)
    sel_spec = pl.BlockSpec((PEER_HEADS, N_KEYS, tm), lambda i: (0, 0, i))
    row16 = pltpu.VMEM((ng, TOPK, LANE), F32)
    keyrows = pltpu.VMEM((ng, N_KEYS, LANE), F32)
    return pl.pallas_call(
        _peer_topk_kernel,
        out_shape=(sel(BF16), sel(F32), sel(F32), sel(BF16)),
        grid=(t // tm,),
        in_specs=[pl.BlockSpec((tm, d), lambda i: (i, 0)), _full(w_pq), _full(keys)],
        out_specs=(sel_spec, sel_spec, sel_spec, sel_spec),
        scratch_shapes=[pltpu.VMEM((2 * PEER_HEADS, N_KEYS, tm), F32), row16, row16,
                        pltpu.VMEM((ng, 64, LANE), F32), row16, row16, keyrows, keyrows],
        compiler_params=_params("parallel"),
        name="peer_topk",
    )(hb, w_pq, keys)


def _peer_dense_kernel(*refs, alpha):
    u_refs, vt_refs = refs[:DMA_SPLIT], refs[DMA_SPLIT:2 * DMA_SPLIT]
    (hb_ref, h_ref, rank2_ref, rfull_ref, e1_ref, e2_ref, g_ref, b_ref,
     o_ref, yt_ref, at_ref, w_ref, r2s_ref, e2s_ref) = refs[2 * DMA_SPLIT:]
    e = pl.program_id(1)
    te, tm = at_ref.shape
    ni = te // N_KEYS
    hw = min(MXU_N, tm)
    halves = [pl.ds(c * hw, hw) for c in range(tm // hw)]

    @pl.when(e == 0)
    def _():
        yt_ref[...] = jnp.zeros_like(yt_ref)
        r2s_ref[...] = rank2_ref[...]
        e2s_ref[...] = e2_ref[...]

    ue = te // DMA_SPLIT
    vd = yt_ref.shape[0] // DMA_SPLIT
    for hl in halves:
        for k, u_ref in enumerate(u_refs):
            at_ref[k * ue:(k + 1) * ue, hl] = lax.dot_general(u_ref[...], hb_ref[hl, :], _NT,
                                                              preferred_element_type=F32)

    igroup = pl.ds(pl.multiple_of(e * ni, SUBLANE), ni)
    for c, hl in enumerate(halves):
        for lg in range(hw // LANE):
            lanes = pl.ds(c * hw + lg * LANE, LANE)
            for il in range(ni):
                rows = pl.ds(il * N_KEYS, N_KEYS)
                gate = None
                for h in range(PEER_HEADS):
                    bcast = lambda ref: jnp.broadcast_to(ref[h, igroup, lanes][il:il + 1, :].astype(BF16), (N_KEYS, LANE))
                    e2 = e2s_ref[h, :, lanes]
                    term = jnp.where(r2s_ref[h, :, lanes] <= bcast(rfull_ref), e2 * bcast(e1_ref), jnp.zeros_like(e2))
                    gate = term if gate is None else gate + term
                act = jax.nn.gelu(at_ref[rows, lanes], approximate=True).astype(BF16)
                w_ref[rows, lanes] = act * gate
        for r, vt_ref in enumerate(vt_refs):
            yt_ref[r * vd:(r + 1) * vd, hl] += jnp.dot(vt_ref[...], w_ref[:, hl], preferred_element_type=F32)

    @pl.when(e == pl.num_programs(1) - 1)
    def _():
        o_ref[...] = _layernorm(alpha * h_ref[...] + yt_ref[...].T, g_ref[...], b_ref[...])


def _expert_tables(expert_u, expert_v):
    depth, ne, d = expert_u.shape
    u = expert_u.reshape(depth * ne, d).astype(BF16)
    vt = expert_v.astype(BF16).reshape(depth * ne // PEER_TE, PEER_TE, DMA_SPLIT, d // DMA_SPLIT)
    return u, vt.transpose(0, 2, 3, 1)


def _peer_dense(u, vt, layer, hb, h, sel, ln_g, ln_b, alpha, tm=512):
    t, d = h.shape
    te = PEER_TE
    n_tiles = N_KEYS * N_KEYS // te
    tm = min(tm, t)
    tile = lambda e: layer * n_tiles + e
    u_specs = [pl.BlockSpec((te // DMA_SPLIT, d), functools.partial(lambda k, i, e: (tile(e) * DMA_SPLIT + k, 0), k))
               for k in range(DMA_SPLIT)]
    vt_specs = [pl.BlockSpec((None, None, d // DMA_SPLIT, te), functools.partial(lambda r, i, e: (tile(e), r, 0, 0), r))
                for r in range(DMA_SPLIT)]
    once = pl.Buffered(1)
    sel_spec = pl.BlockSpec((PEER_HEADS, N_KEYS, tm), lambda i, e: (0, 0, i), pipeline_mode=once)
    tok_in = pl.BlockSpec((tm, d), lambda i, e: (i, 0), pipeline_mode=once)
    tok = pl.BlockSpec((tm, d), lambda i, e: (i, 0))
    vec = pl.BlockSpec((1, d), lambda i, e: (0, 0))
    return pl.pallas_call(
        functools.partial(_peer_dense_kernel, alpha=alpha),
        out_shape=jax.ShapeDtypeStruct((t, d), F32),
        grid=(t // tm, n_tiles),
        in_specs=u_specs + vt_specs + [tok_in, tok_in, sel_spec, sel_spec, sel_spec, sel_spec, vec, vec],
        out_specs=tok,
        scratch_shapes=[pltpu.VMEM((d, tm), F32), pltpu.VMEM((te, tm), F32), pltpu.VMEM((te, tm), BF16),
                        pltpu.VMEM((PEER_HEADS, N_KEYS, tm), BF16), pltpu.VMEM((PEER_HEADS, N_KEYS, tm), BF16)],
        compiler_params=_params("parallel", "arbitrary"),
        name="peer_dense_ln2",
    )(*([u] * DMA_SPLIT), *([vt] * DMA_SPLIT), hb, h, *sel, ln_g, ln_b)


def _ple_kernel(h_ref, pe_ref, wg_ref, wp_ref, o_ref):
    h = h_ref[...]
    gate = jnp.dot(h.astype(BF16), wg_ref[...], preferred_element_type=F32)
    emb = jnp.dot(pe_ref[...].astype(BF16), wp_ref[...], preferred_element_type=F32)
    o_ref[...] = h + (1.0 / (1.0 + jnp.exp(-gate))) * emb


def _ple(h, pe, w_gate, w_ple, tm=256):
    t, d = h.shape
    tm = min(tm, t)
    p = pe.shape[1]
    return pl.pallas_call(
        _ple_kernel,
        out_shape=jax.ShapeDtypeStruct((t, d), F32),
        grid=(t // tm,),
        in_specs=[pl.BlockSpec((tm, d), lambda i: (i, 0)), pl.BlockSpec((tm, p), lambda i: (i, 0)),
                  _full(w_gate), _full(w_ple)],
        out_specs=pl.BlockSpec((tm, d), lambda i: (i, 0)),
        compiler_params=_params("parallel"),
        name="ple_gate",
    )(h, pe, w_gate, w_ple)


def _layer(x, pe, past_k, past_v, s0, wts, alpha):
    batch, seq, d = x.shape
    t = batch * seq
    x2 = x.reshape(t, d)
    k_sb, v_sb, kb, vb, rest, lr = _in_proj(x2, wts["w_in"], wts["w_in_lr"])
    if past_k is None:
        o_sb = _sb_prompt(rest, kb, vb, wts["g_sb"], batch, seq)
    else:
        past = past_k.shape[1]
        o_sb = _sb_sample(rest, kb, vb, past_k.reshape(batch, past, SB_W), past_v.reshape(batch, past, SB_W),
                          wts["g_sb"], batch, seq)
    o_g, s_new = _gla(rest, lr, wts["w_lr"], wts["b_lr"], wts["g_gla"], s0, batch, seq)
    h1, h1b = _outproj(o_sb, o_g, x2, wts["w_out"], wts["ln1_g"], wts["ln1_b"], alpha)
    sel = _peer_topk(h1b, wts["w_pq"], wts["keys"])
    h2 = _peer_dense(wts["u"], wts["vt"], wts["layer"], h1b, h1, sel, wts["ln2_g"], wts["ln2_b"], alpha)
    out = _ple(h2, pe.reshape(t, -1), wts["w_gate"], wts["w_ple"])
    heads = lambda a: a.reshape(batch, seq, SB_HEADS, SB_D)
    return out.reshape(batch, seq, d), heads(k_sb), heads(v_sb), s_new


def _prep_weights(w_in, w_gla_lr, b_gla_lr, g_sb_norm, g_gla_norm, w_out, ln1_g, ln1_b,
                  w_pq, peer_keys, ln2_g, ln2_b, w_ple, w_ple_gate):
    d = w_in.shape[0]
    lr0 = 3 * SB_W + 2 * GLA_K_W + GLA_V_W
    w_main = jnp.concatenate([w_in[:, SB_W:3 * SB_W], w_in[:, :SB_W], w_in[:, 3 * SB_W:lr0],
                              w_in[:, lr0 + GLA_LR:]], axis=1).astype(BF16)
    w_main = w_main.reshape(d, -1, PROJ_TILE).transpose(1, 0, 2)
    w_in_lr = jnp.concatenate([w_in[:, lr0:lr0 + GLA_LR], jnp.zeros((d, LANE - GLA_LR), w_in.dtype)], axis=1).astype(BF16)
    w_lr = jnp.concatenate([w_gla_lr, jnp.zeros((LANE - GLA_LR, GLA_K_W), w_gla_lr.dtype)], axis=0).astype(BF16)
    row = lambda a: a.reshape(1, -1)
    return {
        "w_in": w_main, "w_in_lr": w_in_lr, "w_lr": w_lr, "b_lr": row(b_gla_lr),
        "g_sb": row(g_sb_norm), "g_gla": row(g_gla_norm),
        "w_out": w_out.astype(BF16), "ln1_g": row(ln1_g), "ln1_b": row(ln1_b),
        "w_pq": w_pq.astype(BF16), "keys": peer_keys.reshape(2 * PEER_HEADS, N_KEYS, PEER_HALF).astype(BF16),
        "ln2_g": row(ln2_g), "ln2_b": row(ln2_b), "w_ple": w_ple.astype(BF16), "w_gate": w_ple_gate.astype(BF16),
    }


def kernel(x_prompt, x_sample, cache_sb_k, cache_sb_v, state_gla, p_prompt, p_sample, w_in, w_gla_lr, b_gla_lr, g_sb_norm, g_gla_norm, w_out, ln1_g, ln1_b, w_pq, peer_keys, expert_u, expert_v, ln2_g, ln2_b, w_ple, w_ple_gate):
    depth = w_in.shape[0]
    alpha = (2.0 * depth) ** 0.25
    hp, hs = x_prompt, x_sample
    outs = [[] for _ in range(6)]
    u, vt = _expert_tables(expert_u, expert_v)
    for i in range(depth):
        wts = _prep_weights(w_in[i], w_gla_lr[i], b_gla_lr[i], g_sb_norm[i], g_gla_norm[i], w_out[i],
                            ln1_g[i], ln1_b[i], w_pq[i], peer_keys[i], ln2_g[i], ln2_b[i], w_ple[i], w_ple_gate[i])
        wts.update(u=u, vt=vt, layer=i)
        s0p = jnp.zeros((hp.shape[0], GLA_HEADS, GLA_DK, GLA_DV), F32)
        hp, kp, vp, sp = _layer(hp, p_prompt[i], None, None, s0p, wts, alpha)
        hs, ks, vs, ss = _layer(hs, p_sample[i], cache_sb_k[i], cache_sb_v[i], state_gla[i], wts, alpha)
        for lst, val in zip(outs, (kp, vp, sp, ks, vs, ss)):
            lst.append(val)
    return (hp, hs) + tuple(jnp.stack(lst) for lst in outs)
```

```python
import functools

import jax
import jax.numpy as jnp
from jax import lax
from jax.experimental import pallas as pl
from jax.experimental.pallas import tpu as pltpu

BF16 = jnp.bfloat16
F32 = jnp.float32

SB_HEADS = 8
SB_D = 128
SB_W = SB_HEADS * SB_D
GLA_HEADS = 4
GLA_DK = 128
GLA_DV = 256
GLA_K_W = GLA_HEADS * GLA_DK
GLA_V_W = GLA_HEADS * GLA_DV
GLA_LR = 16
GATE_TAU = 16.0
CHUNK = 64
PEER_HEADS = 8
PEER_HALF = 128
N_KEYS = 128
TOPK = 16
EPS = 1e-6

LANE = 128
SUBLANE = 8
BF16_ROWS = 16
MXU_N = 256
VMEM_LIMIT = 56 * 1024 * 1024

COL_QSB = 0
COL_QG = SB_W // LANE
COL_KG = COL_QG + GLA_K_W // LANE
COL_VG = COL_KG + GLA_K_W // LANE
COL_RG = COL_VG + GLA_V_W // LANE
REST_W = SB_W + 2 * GLA_K_W + 2 * GLA_V_W
PROJ_TILE = 1024

PEER_TE = SUBLANE * N_KEYS

EXP_ZERO_BELOW = -88.0

CODE_BASE = -(2.0 ** 127)
CODE_STEP = 2.0 ** 110
CODE_BELOW = -(2.0 ** 126)

_NT = (((1,), (1,)), ((), ()))
_TN = (((0,), (0,)), ((), ()))


def _params(*sem):
    return pltpu.CompilerParams(dimension_semantics=sem, vmem_limit_bytes=VMEM_LIMIT)


def _full(a):
    return pl.BlockSpec(a.shape, lambda *_: (0,) * a.ndim)


def _in_proj_kernel(x_ref, w_ref, wlr_ref, k_ref, v_ref, kb_ref, vb_ref, rest_ref, lr_ref, xb_ref):
    j = pl.program_id(1)
    dot = lambda w: jnp.dot(xb_ref[...], w[...], preferred_element_type=F32)

    def put_heads(ref, acc):
        for h in range(SB_HEADS):
            ref[:, h, :] = acc[:, h * SB_D:(h + 1) * SB_D]

    @pl.when(j == 0)
    def _():
        xb_ref[...] = x_ref[...].astype(BF16)
        lr_ref[...] = dot(wlr_ref)
        acc = dot(w_ref)
        put_heads(k_ref, acc)
        kb_ref[...] = acc.astype(BF16)

    @pl.when(j == 1)
    def _():
        acc = dot(w_ref)
        put_heads(v_ref, acc)
        vb_ref[...] = acc.astype(BF16)

    @pl.when(j >= 2)
    def _():
        rest_ref[...] = dot(w_ref)


def _in_proj(x, w_main, w_lr, tm=512):
    t, d = x.shape
    tm = min(tm, t)
    n_tiles = w_main.shape[0]
    tok = lambda w: pl.BlockSpec((tm, w), lambda i, j: (i, 0))
    sds = lambda w, dt: jax.ShapeDtypeStruct((t, w), dt)
    heads = jax.ShapeDtypeStruct((t, SB_HEADS, SB_D), F32)
    tok_heads = pl.BlockSpec((tm, SB_HEADS, SB_D), lambda i, j: (i, 0, 0))
    return pl.pallas_call(
        _in_proj_kernel,
        out_shape=(heads, heads, sds(SB_W, BF16), sds(SB_W, BF16), sds(REST_W, F32), sds(LANE, F32)),
        grid=(t // tm, n_tiles),
        in_specs=[tok(d), pl.BlockSpec((None, d, PROJ_TILE), lambda i, j: (j, 0, 0)), pl.BlockSpec((d, LANE), lambda i, j: (0, 0))],
        out_specs=(tok_heads, tok_heads, tok(SB_W), tok(SB_W),
                   pl.BlockSpec((tm, PROJ_TILE), lambda i, j: (i, jnp.maximum(j - 2, 0))), tok(LANE)),
        scratch_shapes=[pltpu.VMEM((tm, d), BF16)],
        compiler_params=_params("parallel", "arbitrary"),
        name="in_proj",
    )(x, w_main, w_lr)


def _logsig_pair(z):
    l1p = jnp.log1p(jnp.exp(-jnp.abs(z)))
    return jnp.minimum(z, 0.0) - l1p, -jnp.maximum(z, 0.0) - l1p


def _split3(x):
    hi = x.astype(BF16)
    r = x - hi.astype(F32)
    mid = r.astype(BF16)
    lo = (r - mid.astype(F32)).astype(BF16)
    return hi, mid, lo


def _tri(shape, fn):
    row = lax.broadcasted_iota(jnp.int32, shape, 0)
    col = lax.broadcasted_iota(jnp.int32, shape, 1)
    return fn(row, col)


def _sb_blocks(qbs, ks, vs, carries, mask, mtri):
    zs = [lax.dot_general(q, k.astype(BF16), _NT, preferred_element_type=F32) * (SB_D ** -0.5)
          for q, k in zip(qbs, ks)]
    pairs = [_logsig_pair(z) for z in zs]
    lfs = [lf if mask is None else jnp.where(mask, lf, 0.0) for _, lf in pairs]
    splits = [_split3(lf) for lf in lfs]
    dot = lambda a: jnp.dot(a, mtri, preferred_element_type=F32)
    cums = [(dot(lo) + dot(mid)) + dot(hi) for hi, mid, lo in splits]
    ws = [jnp.exp(ls + (c + cum)) for (ls, _), c, cum in zip(pairs, carries, cums)]
    if mask is not None:
        ws = [jnp.where(mask, w, 0.0) for w in ws]
    pvs = [jnp.dot(w.astype(BF16), v.astype(BF16), preferred_element_type=F32) for w, v in zip(ws, vs)]
    return pvs, [c + jnp.sum(lf, axis=1, keepdims=True) for c, lf in zip(carries, lfs)]


def _sb_attend(q_ref, kd_ref, vd_ref, d0, kp_ref, vp_ref, n_past, tk, g, o_ref, acc_ref, carry_ref):
    tq = q_ref.shape[0]
    heads = range(SB_HEADS)
    cols = lambda h: slice(h * SB_D, (h + 1) * SB_D)
    qbs = [q_ref[:, cols(h)].astype(BF16) for h in heads]
    mask = _tri((tq, tq), lambda r, c: c < r)
    mtri_d = _tri((tq, tq), lambda r, c: jnp.where(r > c, 1.0, 0.0)).astype(BF16)
    drows = pl.ds(d0, tq)
    pvs, carries = _sb_blocks(qbs, [kd_ref[drows, cols(h)] for h in heads], [vd_ref[drows, cols(h)] for h in heads],
                              [jnp.zeros((tq, 1), F32)] * SB_HEADS, mask, mtri_d)
    for h in heads:
        acc_ref[:, cols(h)] = pvs[h]
        carry_ref[h] = carries[h]
    mtri = _tri((tk, tk), lambda r, c: jnp.where(r > c, 1.0, 0.0)).astype(BF16)

    def cond(st):
        kb, mx = st
        return jnp.logical_and(kb >= 0, mx > EXP_ZERO_BELOW)

    def body(st):
        kb, _ = st
        rows = pl.ds(pl.multiple_of(kb * tk, tk), tk)
        block = lambda ref, h: ref[rows, h, :] if len(ref.shape) == 3 else ref[rows, cols(h)]
        pvs, carries = _sb_blocks(qbs, [block(kp_ref, h) for h in heads], [block(vp_ref, h) for h in heads],
                                  [carry_ref[h] for h in heads], None, mtri)
        for h in heads:
            acc_ref[:, cols(h)] += pvs[h]
            carry_ref[h] = carries[h]
        return kb - 1, jnp.max(functools.reduce(jnp.maximum, carries))

    lax.while_loop(cond, body, (n_past - 1, jnp.max(functools.reduce(jnp.maximum, carries))))
    for h in heads:
        o = acc_ref[:, cols(h)]
        o_ref[:, cols(h)] = o * lax.rsqrt(jnp.mean(o * o, axis=-1, keepdims=True) + EPS) * g


def _sb_prompt_kernel(q_ref, k_ref, v_ref, g_ref, o_ref, acc_ref, carry_ref):
    tq = q_ref.shape[0]
    qi = pl.program_id(1)
    _sb_attend(q_ref, k_ref, v_ref, pl.multiple_of(qi * tq, tq), k_ref, v_ref, qi, tq, g_ref[...],
               o_ref, acc_ref, carry_ref)


def _sb_scratch(tq):
    return [pltpu.VMEM((tq, SB_W), F32), pltpu.VMEM((SB_HEADS, tq, 1), F32)]


def _sb_prompt(rest, kb, vb, g_sb, batch, seq, tq=128):
    nq = seq // tq
    whole = pl.BlockSpec((seq, SB_W), lambda b, i: (b, 0), pipeline_mode=pl.Buffered(1))
    return pl.pallas_call(
        _sb_prompt_kernel,
        out_shape=jax.ShapeDtypeStruct((batch * seq, SB_W), F32),
        grid=(batch, nq),
        in_specs=[pl.BlockSpec((tq, SB_W), lambda b, i: (b * nq + i, COL_QSB)), whole, whole, _full(g_sb)],
        out_specs=pl.BlockSpec((tq, SB_W), lambda b, i: (b * nq + i, 0)),
        scratch_shapes=_sb_scratch(tq),
        compiler_params=_params("parallel", "arbitrary"),
        name="sb_prompt",
    )(rest, kb, vb, g_sb)


def _sb_sample_kernel(q_ref, kn_ref, vn_ref, kp_ref, vp_ref, g_ref, o_ref, acc_ref, carry_ref, *, tk):
    _sb_attend(q_ref, kn_ref, vn_ref, 0, kp_ref, vp_ref, kp_ref.shape[0] // tk, tk, g_ref[...],
               o_ref, acc_ref, carry_ref)


def _sb_sample(rest, kb, vb, cache_k, cache_v, layer, g_sb, batch, seq, tk=128):
    past = cache_k.shape[2]
    new = pl.BlockSpec((seq, SB_W), lambda b: (b, 0))
    old = pl.BlockSpec((None, None, past, SB_HEADS, SB_D), lambda b: (layer, b, 0, 0, 0))
    return pl.pallas_call(
        functools.partial(_sb_sample_kernel, tk=tk),
        out_shape=jax.ShapeDtypeStruct((batch * seq, SB_W), F32),
        grid=(batch,),
        in_specs=[pl.BlockSpec((seq, SB_W), lambda b: (b, COL_QSB)), new, new, old, old, _full(g_sb)],
        out_specs=new,
        scratch_shapes=_sb_scratch(seq),
        compiler_params=_params("parallel"),
        name="sb_sample",
    )(rest, kb, vb, cache_k, cache_v, g_sb)


def _gla_kernel(q_ref, k_ref, v_ref, r_ref, lr_ref, wlr_ref, blr_ref, gn_ref, s0_ref, o_ref, sout_ref,
                st_ref, sj_ref, *, c):
    si = pl.program_id(2)
    rows = q_ref.shape[0]
    n_chunks = rows // c
    shift = c.bit_length() - 1
    assert c == 1 << shift

    @pl.when(si == 0)
    def _():
        st_ref[...] = s0_ref[...].T

    causal = _tri((rows, rows), lambda r, cc: jnp.logical_and((r >> shift) == (cc >> shift), cc <= r))
    ltri = jnp.where(causal, 1.0, 0.0).astype(BF16)
    pre = jnp.dot(lr_ref[...].astype(BF16), wlr_ref[...], preferred_element_type=F32) + blr_ref[...]
    la = (jnp.minimum(pre, 0.0) - jnp.log1p(jnp.exp(-jnp.abs(pre)))) * (1.0 / GATE_TAU)
    hi, mid, lo = _split3(la)
    dot = lambda a: jnp.dot(ltri, a, preferred_element_type=F32)
    b = (dot(lo) + dot(mid)) + dot(hi)
    k = k_ref[...]
    vb = v_ref[...].astype(BF16)
    qe = (q_ref[...] * (GLA_DK ** -0.5) * jnp.exp(b)).astype(BF16)
    ke = (k * jnp.exp(-b)).astype(BF16)
    att = lax.dot_general(qe, ke, _NT, preferred_element_type=F32)
    att = jnp.where(causal, att, 0.0).astype(BF16)
    o_intra = jnp.dot(att, vb, preferred_element_type=F32)

    st = st_ref[...]
    for j in range(n_chunks):
        sl = slice(j * c, (j + 1) * c)
        sj_ref[j] = st.astype(BF16)
        b_last = b[(j + 1) * c - 1:(j + 1) * c, :]
        kd = (k[sl, :] * jnp.exp(b_last - b[sl, :])).astype(BF16)
        st = st * jnp.exp(b_last) + lax.dot_general(vb[sl, :], kd, _TN, preferred_element_type=F32)
    st_ref[...] = st

    for j in range(n_chunks):
        sl = slice(j * c, (j + 1) * c)
        o = o_intra[sl, :] + lax.dot_general(qe[sl, :], sj_ref[j], _NT, preferred_element_type=F32)
        on = o * lax.rsqrt(jnp.mean(o * o, axis=-1, keepdims=True) + EPS) * gn_ref[...]
        r = r_ref[sl, :]
        o_ref[sl, :] = on * (r * (1.0 / (1.0 + jnp.exp(-r))))

    @pl.when(si == pl.num_programs(2) - 1)
    def _():
        sout_ref[...] = st_ref[...].T


def _gla(rest, lr, wlr, blr, g_gla, s0, batch, seq):
    c = CHUNK if seq % CHUNK == 0 else seq
    rows = min(seq, 8 * c)
    ns = seq // rows
    row = lambda b, h, s: b * ns + s
    return pl.pallas_call(
        functools.partial(_gla_kernel, c=c),
        out_shape=(jax.ShapeDtypeStruct((batch * seq, GLA_V_W), F32),
                   jax.ShapeDtypeStruct((batch, GLA_HEADS, GLA_DK, GLA_DV), F32)),
        grid=(batch, GLA_HEADS, ns),
        in_specs=[
            pl.BlockSpec((rows, GLA_DK), lambda b, h, s: (row(b, h, s), COL_QG + h)),
            pl.BlockSpec((rows, GLA_DK), lambda b, h, s: (row(b, h, s), COL_KG + h)),
            pl.BlockSpec((rows, GLA_DV), lambda b, h, s: (row(b, h, s), COL_VG // 2 + h)),
            pl.BlockSpec((rows, GLA_DV), lambda b, h, s: (row(b, h, s), COL_RG // 2 + h)),
            pl.BlockSpec((rows, LANE), lambda b, h, s: (row(b, h, s), 0)),
            pl.BlockSpec((LANE, GLA_DK), lambda b, h, s: (0, h)),
            pl.BlockSpec((1, GLA_DK), lambda b, h, s: (0, h)),
            pl.BlockSpec((1, GLA_DV), lambda b, h, s: (0, 0)),
            pl.BlockSpec((None, None, GLA_DK, GLA_DV), lambda b, h, s: (b, h, 0, 0)),
        ],
        out_specs=(pl.BlockSpec((rows, GLA_DV), lambda b, h, s: (row(b, h, s), h)),
                   pl.BlockSpec((None, None, GLA_DK, GLA_DV), lambda b, h, s: (b, h, 0, 0))),
        scratch_shapes=[pltpu.VMEM((GLA_DV, GLA_DK), F32), pltpu.VMEM((rows // c, GLA_DV, GLA_DK), BF16)],
        compiler_params=_params("parallel", "parallel", "arbitrary"),
        name="gla",
    )(rest, rest, rest, rest, lr, wlr, blr, g_gla, s0)


def _layernorm(y, g, b):
    mu = jnp.mean(y, axis=-1, keepdims=True)
    d = y - mu
    var = jnp.mean(d * d, axis=-1, keepdims=True)
    return d * lax.rsqrt(var + EPS) * g + b


def _outproj_kernel(osb_ref, og_ref, x_ref, w_ref, g_ref, b_ref, h_ref, hb_ref, *, alpha):
    acc = jnp.dot(osb_ref[...].astype(BF16), w_ref[0:SB_W, :], preferred_element_type=F32)
    acc = acc + jnp.dot(og_ref[...].astype(BF16), w_ref[SB_W:SB_W + GLA_V_W, :], preferred_element_type=F32)
    h = _layernorm(alpha * x_ref[...] + acc, g_ref[...], b_ref[...])
    h_ref[...] = h
    hb_ref[...] = h.astype(BF16)


def _outproj(o_sb, o_g, x, w_out, ln_g, ln_b, alpha, tm=256):
    t, d = x.shape
    tm = min(tm, t)
    tok = lambda w: pl.BlockSpec((tm, w), lambda i: (i, 0))
    return pl.pallas_call(
        functools.partial(_outproj_kernel, alpha=alpha),
        out_shape=(jax.ShapeDtypeStruct((t, d), F32), jax.ShapeDtypeStruct((t, d), BF16)),
        grid=(t // tm,),
        in_specs=[tok(SB_W), tok(GLA_V_W), tok(d), _full(w_out), _full(ln_g), _full(ln_b)],
        out_specs=(tok(d), tok(d)),
        compiler_params=_params("parallel"),
        name="out_proj_ln1",
    )(o_sb, o_g, x, w_out, ln_g, ln_b)


def _top16(s, vals_ref, exact):
    iota = lax.broadcasted_iota(jnp.int32, s.shape, 0).astype(F32) if exact else None
    for r in range(TOPK):
        m = jnp.max(s, axis=0, keepdims=True)
        sel = s == m
        if exact:
            first = jnp.min(jnp.where(sel, iota, float(s.shape[0])), axis=0, keepdims=True)
            sel = iota == first
        s = jnp.where(sel, CODE_BASE + (r + 1) * CODE_STEP, s)
        vals_ref[r:r + 1, :] = m
    coded = s < CODE_BELOW
    rank = jnp.where(coded, (s - CODE_BASE) * (1.0 / CODE_STEP), float(TOPK + 1))
    count = jnp.sum(jnp.where(coded, 1.0, 0.0), axis=0, keepdims=True)
    return rank, count


def _peer_select(s1, s2, a_ref, b_ref, c_ref, best_ref, rv_ref, rk1_ref, rk2_ref, exact):
    rank1, n1 = _top16(s1, a_ref, exact)
    rank2, n2 = _top16(s2, b_ref, exact)
    rk1_ref[...] = rank1
    rk2_ref[...] = rank2
    c_ref[0:8, :] = a_ref[0:1, :] + b_ref[0:8, :]
    c_ref[8:16, :] = a_ref[0:1, :] + b_ref[8:16, :]
    c_ref[16:24, :] = a_ref[1:2, :] + b_ref[0:8, :]
    c_ref[24:32, :] = a_ref[2:3, :] + b_ref[0:8, :]
    c_ref[32:40, :] = a_ref[3:4, :] + b_ref[0:8, :]
    c_ref[40:44, :] = a_ref[4:5, :] + b_ref[0:4, :]
    c_ref[44:48, :] = a_ref[5:6, :] + b_ref[0:4, :]
    c_ref[48:52, :] = a_ref[6:7, :] + b_ref[0:4, :]
    c_ref[52:56, :] = a_ref[7:8, :] + b_ref[0:4, :]
    c_ref[56:64, :] = a_ref[8:16, :] + b_ref[0:1, :]
    crank, n3 = _top16(c_ref[...], best_ref, exact)
    c_ref[...] = jnp.where(crank <= float(TOPK), 1.0, 0.0)
    rsum = lambda lo, hi: jnp.sum(c_ref[lo:hi, :], axis=0, keepdims=True)
    rv_ref[0:1, :] = rsum(0, 16)
    rv_ref[1:2, :] = rsum(16, 24)
    rv_ref[2:3, :] = rsum(24, 32)
    rv_ref[3:4, :] = rsum(32, 40)
    rv_ref[4:5, :] = rsum(40, 44)
    rv_ref[5:6, :] = rsum(44, 48)
    rv_ref[6:7, :] = rsum(48, 52)
    rv_ref[7:8, :] = rsum(52, 56)
    rv_ref[8:16, :] = c_ref[56:64, :]
    return jnp.maximum(jnp.maximum(n1, n2), n3)


def _peer_topk_kernel(hb_ref, wpq_ref, keys_ref, rank2_ref, rfull_ref, e1_ref, e2_ref,
                      s_ref, a_ref, b_ref, c_ref, best_ref, rv_ref, rk1_ref, rk2_ref):
    tm = hb_ref.shape[0]
    q = jnp.dot(hb_ref[...], wpq_ref[...], preferred_element_type=F32).astype(BF16)
    for hp in range(2 * PEER_HEADS):
        s_ref[hp] = lax.dot_general(keys_ref[hp], q[:, hp * PEER_HALF:(hp + 1) * PEER_HALF], _NT,
                                    preferred_element_type=F32)

    groups = range(tm // LANE)
    lanes = [pl.ds(g * LANE, LANE) for g in groups]
    scratch = lambda g: [ref.at[g] for ref in (a_ref, b_ref, c_ref, best_ref, rv_ref, rk1_ref, rk2_ref)]

    def head(h, carry):
        s1 = [s_ref[2 * h, :, lanes[g]] for g in groups]
        s2 = [s_ref[2 * h + 1, :, lanes[g]] for g in groups]
        counts = [_peer_select(s1[g], s2[g], *scratch(g), exact=False) for g in groups]

        @pl.when(jnp.max(functools.reduce(jnp.maximum, counts)) > TOPK + 0.5)
        def _():
            for g in groups:
                _peer_select(s1[g], s2[g], *scratch(g), exact=True)

        for g in groups:
            rank1 = rk1_ref[g]
            rfull = jnp.zeros((N_KEYS, LANE), F32)
            for r in range(TOPK):
                rfull = jnp.where(rank1 == float(r + 1), rv_ref[g, r:r + 1, :], rfull)
            best = best_ref[g]
            zsum = jnp.sum(jnp.exp(best - best[0:1, :]), axis=0, keepdims=True)
            rank2_ref[h, :, lanes[g]] = rk2_ref[g].astype(BF16)
            rfull_ref[h, :, lanes[g]] = rfull
            e1_ref[h, :, lanes[g]] = jnp.exp(s1[g] - a_ref[g, 0:1, :]) / zsum
            e2_ref[h, :, lanes[g]] = jnp.exp(s2[g] - b_ref[g, 0:1, :]).astype(BF16)
        return carry

    lax.fori_loop(0, PEER_HEADS, head, 0)


def _peer_topk(hb, w_pq, keys, tm=512):
    t, d = hb.shape
    tm = min(tm, t)
    ng = tm // LANE
    sel = lambda dt: jax.ShapeDtypeStruct((PEER_HEADS, N_KEYS, t), dt)
    sel_spec = pl.BlockSpec((PEER_HEADS, N_KEYS, tm), lambda i: (0, 0, i))
    row16 = pltpu.VMEM((ng, TOPK, LANE), F32)
    keyrows = pltpu.VMEM((ng, N_KEYS, LANE), F32)
    return pl.pallas_call(
        _peer_topk_kernel,
        out_shape=(sel(BF16), sel(F32), sel(F32), sel(BF16)),
        grid=(t // tm,),
        in_specs=[pl.BlockSpec((tm, d), lambda i: (i, 0)), _full(w_pq), _full(keys)],
        out_specs=(sel_spec, sel_spec, sel_spec, sel_spec),
        scratch_shapes=[pltpu.VMEM((2 * PEER_HEADS, N_KEYS, tm), F32), row16, row16,
                        pltpu.VMEM((ng, 64, LANE), F32), row16, row16, keyrows, keyrows],
        compiler_params=_params("parallel"),
        name="peer_topk",
    )(hb, w_pq, keys)


def _peer_dense_kernel(u_ref, vt_ref, hb_ref, rank2_ref, rfull_ref, e1_ref, e2_ref, o_ref,
                       at_ref, w_ref, r2s_ref, e2s_ref):
    e = pl.program_id(1)
    te = u_ref.shape[0]
    tm = o_ref.shape[1]
    ni = te // N_KEYS
    cw = min(MXU_N, tm)

    @pl.when(e == 0)
    def _():
        o_ref[...] = jnp.zeros_like(o_ref)
        r2s_ref[...] = rank2_ref[...]
        e2s_ref[...] = e2_ref[...]

    igroup = pl.ds(pl.multiple_of(e * ni, SUBLANE), ni)
    n_chunks = tm // cw

    def scores(c):
        at_ref[c % 2] = lax.dot_general(u_ref[...], hb_ref[pl.ds(c * cw, cw), :], _NT, preferred_element_type=F32)

    scores(0)
    for c in range(n_chunks):
        chunk = pl.ds(c * cw, cw)
        buf = c % 2
        if c + 1 < n_chunks:
            scores(c + 1)
        for lg in range(cw // LANE):
            lanes = pl.ds(c * cw + lg * LANE, LANE)
            sub = pl.ds(lg * LANE, LANE)
            for il in range(ni):
                rows = pl.ds(il * N_KEYS, N_KEYS)
                gate = None
                for h in range(PEER_HEADS):
                    bcast = lambda ref: jnp.broadcast_to(ref[h, igroup, lanes][il:il + 1, :].astype(BF16), (N_KEYS, LANE))
                    e2 = e2s_ref[h, :, lanes]
                    term = jnp.where(r2s_ref[h, :, lanes] <= bcast(rfull_ref), e2 * bcast(e1_ref), jnp.zeros_like(e2))
                    gate = term if gate is None else gate + term
                act = jax.nn.gelu(at_ref[buf, rows, sub], approximate=True).astype(BF16)
                w_ref[buf, rows, sub] = act * gate
        o_ref[:, chunk] += jnp.dot(vt_ref[...], w_ref[buf], preferred_element_type=F32)


def _expert_tables(expert_u, expert_v):
    depth, ne, d = expert_u.shape
    u = expert_u.reshape(depth * ne, d).astype(BF16)
    vt = expert_v.astype(BF16).reshape(depth * ne // PEER_TE, PEER_TE, d)
    return u, vt.transpose(0, 2, 1)


def _peer_dense(u, vt, layer, hb, sel, tm=1024):
    t, d = hb.shape
    te = PEER_TE
    n_tiles = N_KEYS * N_KEYS // te
    tm = min(tm, t)
    cw = min(MXU_N, tm)
    once = pl.Buffered(1)
    sel_spec = pl.BlockSpec((PEER_HEADS, N_KEYS, tm), lambda i, e: (0, 0, i), pipeline_mode=once)
    packed = pltpu.VMEM((PEER_HEADS, N_KEYS, tm), BF16)
    return pl.pallas_call(
        _peer_dense_kernel,
        out_shape=jax.ShapeDtypeStruct((d, t), F32),
        grid=(t // tm, n_tiles),
        in_specs=[pl.BlockSpec((te, d), lambda i, e: (layer * n_tiles + e, 0)),
                  pl.BlockSpec((None, d, te), lambda i, e: (layer * n_tiles + e, 0, 0)),
                  pl.BlockSpec((tm, d), lambda i, e: (i, 0), pipeline_mode=once),
                  sel_spec, sel_spec, sel_spec, sel_spec],
        out_specs=pl.BlockSpec((d, tm), lambda i, e: (0, i), pipeline_mode=once),
        scratch_shapes=[pltpu.VMEM((2, te, cw), F32), pltpu.VMEM((2, te, cw), BF16), packed, packed],
        compiler_params=_params("parallel", "arbitrary"),
        name="peer_dense",
    )(u, vt, hb, *sel)


def _ple_kernel(h1_ref, yt_ref, pe_ref, wg_ref, wp_ref, g_ref, b_ref, o_ref, *, alpha):
    h = _layernorm(alpha * h1_ref[...] + yt_ref[...].T, g_ref[...], b_ref[...])
    gate = jnp.dot(h.astype(BF16), wg_ref[...], preferred_element_type=F32)
    emb = jnp.dot(pe_ref[...].astype(BF16), wp_ref[...], preferred_element_type=F32)
    o_ref[...] = h + (1.0 / (1.0 + jnp.exp(-gate))) * emb


def _ple(h1, yt, pe, w_gate, w_ple, ln_g, ln_b, alpha, tm=256):
    t, d = h1.shape
    tm = min(tm, t)
    p = pe.shape[1]
    return pl.pallas_call(
        functools.partial(_ple_kernel, alpha=alpha),
        out_shape=jax.ShapeDtypeStruct((t, d), F32),
        grid=(t // tm,),
        in_specs=[pl.BlockSpec((tm, d), lambda i: (i, 0)), pl.BlockSpec((d, tm), lambda i: (0, i)),
                  pl.BlockSpec((tm, p), lambda i: (i, 0)), _full(w_gate), _full(w_ple), _full(ln_g), _full(ln_b)],
        out_specs=pl.BlockSpec((tm, d), lambda i: (i, 0)),
        compiler_params=_params("parallel"),
        name="ln2_ple_gate",
    )(h1, yt, pe, w_gate, w_ple, ln_g, ln_b)


def _layer(x, pe, cache_k, cache_v, s0, wts, alpha):
    batch, seq, d = x.shape
    t = batch * seq
    x2 = x.reshape(t, d)
    k_sb, v_sb, kb, vb, rest, lr = _in_proj(x2, wts["w_in"], wts["w_in_lr"])
    if cache_k is None:
        o_sb = _sb_prompt(rest, kb, vb, wts["g_sb"], batch, seq)
    else:
        o_sb = _sb_sample(rest, kb, vb, cache_k, cache_v, wts["layer"], wts["g_sb"], batch, seq)
    o_g, s_new = _gla(rest, lr, wts["w_lr"], wts["b_lr"], wts["g_gla"], s0, batch, seq)
    h1, h1b = _outproj(o_sb, o_g, x2, wts["w_out"], wts["ln1_g"], wts["ln1_b"], alpha)
    sel = _peer_topk(h1b, wts["w_pq"], wts["keys"])
    yt = _peer_dense(wts["u"], wts["vt"], wts["layer"], h1b, sel)
    out = _ple(h1, yt, pe.reshape(t, -1), wts["w_gate"], wts["w_ple"], wts["ln2_g"], wts["ln2_b"], alpha)
    heads = lambda a: a.reshape(batch, seq, SB_HEADS, SB_D)
    return out.reshape(batch, seq, d), heads(k_sb), heads(v_sb), s_new


def _prep_weights(w_in, w_gla_lr, b_gla_lr, g_sb_norm, g_gla_norm, w_out, ln1_g, ln1_b,
                  w_pq, peer_keys, ln2_g, ln2_b, w_ple, w_ple_gate):
    d = w_in.shape[0]
    lr0 = 3 * SB_W + 2 * GLA_K_W + GLA_V_W
    w_main = jnp.concatenate([w_in[:, SB_W:3 * SB_W], w_in[:, :SB_W], w_in[:, 3 * SB_W:lr0],
                              w_in[:, lr0 + GLA_LR:]], axis=1).astype(BF16)
    w_main = w_main.reshape(d, -1, PROJ_TILE).transpose(1, 0, 2)
    w_in_lr = jnp.concatenate([w_in[:, lr0:lr0 + GLA_LR], jnp.zeros((d, LANE - GLA_LR), w_in.dtype)], axis=1).astype(BF16)
    w_lr = jnp.concatenate([w_gla_lr, jnp.zeros((LANE - GLA_LR, GLA_K_W), w_gla_lr.dtype)], axis=0).astype(BF16)
    row = lambda a: a.reshape(1, -1)
    return {
        "w_in": w_main, "w_in_lr": w_in_lr, "w_lr": w_lr, "b_lr": row(b_gla_lr),
        "g_sb": row(g_sb_norm), "g_gla": row(g_gla_norm),
        "w_out": w_out.astype(BF16), "ln1_g": row(ln1_g), "ln1_b": row(ln1_b),
        "w_pq": w_pq.astype(BF16), "keys": peer_keys.reshape(2 * PEER_HEADS, N_KEYS, PEER_HALF).astype(BF16),
        "ln2_g": row(ln2_g), "ln2_b": row(ln2_b), "w_ple": w_ple.astype(BF16), "w_gate": w_ple_gate.astype(BF16),
    }


def kernel(x_prompt, x_sample, cache_sb_k, cache_sb_v, state_gla, p_prompt, p_sample, w_in, w_gla_lr, b_gla_lr, g_sb_norm, g_gla_norm, w_out, ln1_g, ln1_b, w_pq, peer_keys, expert_u, expert_v, ln2_g, ln2_b, w_ple, w_ple_gate):
    depth = w_in.shape[0]
    alpha = (2.0 * depth) ** 0.25
    hp, hs = x_prompt, x_sample
    outs = [[] for _ in range(6)]
    u, vt = _expert_tables(expert_u, expert_v)
    for i in range(depth):
        wts = _prep_weights(w_in[i], w_gla_lr[i], b_gla_lr[i], g_sb_norm[i], g_gla_norm[i], w_out[i],
                            ln1_g[i], ln1_b[i], w_pq[i], peer_keys[i], ln2_g[i], ln2_b[i], w_ple[i], w_ple_gate[i])
        wts.update(u=u, vt=vt, layer=i)
        s0p = jnp.zeros((hp.shape[0], GLA_HEADS, GLA_DK, GLA_DV), F32)
        hp, kp, vp, sp = _layer(hp, p_prompt[i], None, None, s0p, wts, alpha)
        hs, ks, vs, ss = _layer(hs, p_sample[i], cache_sb_k, cache_sb_v, state_gla[i], wts, alpha)
        for lst, val in zip(outs, (kp, vp, sp, ks, vs, ss)):
            lst.append(val)
    return (hp, hs) + tuple(jnp.stack(lst) for lst in outs)
```
